```python
import math
import jax
import jax.numpy as jnp
from jax import lax
import numpy as np

D_MODEL = 1024
BATCH = 32
SEQ = 256
DEPTH = 2
DEC_BATCH = 8
DEC_SEQ = 4096
PAST_LEN = 512

GRID_W = 64
N_DIR = 2
HEAD_A = 64
N_HEADS_A = 6
D_A = N_HEADS_A * HEAD_A
RANK_W = 64
RANK_A = 64
RANK_G = 128
GROUP_B = 64
N_GROUPS_B = 4
D_B = N_GROUPS_B * GROUP_B
CONV_B = 3
N_BANDS = 16
POS_DIM = 1 + 2 * N_BANDS
FILTER_HIDDEN = 64
HEAD_C = 64
N_HEADS_C = 6
D_C = N_HEADS_C * HEAD_C
CONV_C = 4
LRU_C = 8.0
D_MIX = D_A + D_B + D_C
COLS_A = 3 * D_A + 2 * RANK_W + 2 * RANK_A + RANK_G
COLS_B = 3 * D_B
COLS_C = 2 * D_C
D_IN = COLS_A + COLS_B + COLS_C
N_EXPERTS = 32
TOP_K = 4
D_FF = 1024
SWIGLU_LIMIT = 7.0
SWIGLU_ALPHA = 1.702
MOE_BLOCK = 128
N_MOD = 6
EPS = 1e-6
LNX_EPS = 64e-5

kernel_name = 'hybrid_rwkv7_hyena_rglru_moe_diffusion_step'


def _split(t, sizes):
    return jnp.split(t, np.cumsum(sizes)[:-1].tolist(), axis=-1)


def rmsnorm(x, g):
    xf = x.astype(jnp.float32)
    xf = xf * lax.rsqrt(jnp.mean(xf * xf, axis=-1, keepdims=True) + EPS)
    return (xf * g).astype(x.dtype)


def short_conv(u, w, left, grid):
    k = w.shape[0]
    if grid:
        bsz, seq, ch = u.shape
        rows = seq // GRID_W
        v = u.reshape(bsz, rows, GRID_W, ch)
        axis = 2
    else:
        v = u
        axis = 1
    pad = [(0, 0)] * v.ndim
    pad[axis] = (left, k - 1 - left)
    vp = jnp.pad(v, pad)
    n = v.shape[axis]
    out = lax.slice_in_dim(vp, 0, n, axis=axis) * w[0]
    for j in range(1, k):
        out = out + lax.slice_in_dim(vp, j, j + n, axis=axis) * w[j]
    return out.reshape(u.shape)


def wkv7_scan(r, w, k, v, a, b, s0, reverse):
    def step(s, inp):
        r_t, w_t, k_t, v_t, a_t, b_t = inp
        sa = jnp.einsum('bhvk,bhk->bhv', s, a_t)
        s = s * w_t[:, :, None, :] + sa[..., None] * b_t[:, :, None, :] + v_t[..., None] * k_t[:, :, None, :]
        return s, jnp.einsum('bhvk,bhk->bhv', s, r_t)
    xs = tuple(jnp.moveaxis(t, 1, 0) for t in (r, w, k, v, a, b))
    s_fin, ys = lax.scan(step, s0, xs, reverse=reverse)
    return s_fin, jnp.moveaxis(ys, 0, 1)


def rwkv7_mixer(p, lp, s0, grid):
    bsz, seq, _ = p.shape
    f32 = jnp.float32
    shift_w = jnp.asarray([[0.5], [0.0], [0.5]], p.dtype)
    p = p + (short_conv(p, shift_w, 1, grid) - p) * lp['rw_mu']
    r, k, v, wl_f, wl_b, al_f, al_b, g_lo = _split(
        p, (D_A, D_A, D_A, RANK_W, RANK_W, RANK_A, RANK_A, RANK_G))

    def heads(t):
        return t.astype(f32).reshape(bsz, seq, N_HEADS_A, HEAD_A)

    kk = heads(k * lp['rw_kk'])
    kk = kk * lax.rsqrt(jnp.sum(kk * kk, axis=-1, keepdims=True) + 1e-12)
    rh, vh, kh = heads(r), heads(v), heads(k)
    ka = lp['rw_ka'].astype(f32).reshape(N_HEADS_A, HEAD_A)
    rk = lp['rw_rk'].astype(f32)
    y_sum, bonus, finals = 0.0, 0.0, []
    for d, (wl, al) in enumerate(((wl_f, al_f), (wl_b, al_b))):
        w_log = -jax.nn.softplus(-(lp['rw_w0'][d] + jnp.tanh(wl) @ lp['rw_w2'][d])) - 0.5
        decay = jnp.exp(-jnp.exp(heads(w_log)))
        a = heads(jax.nn.sigmoid(lp['rw_a0'][d] + al @ lp['rw_a2'][d]))
        kd = kh * (1.0 + (a - 1.0) * ka)
        s_fin, y = wkv7_scan(rh, decay, kd, vh, -kk, kk * a, s0[:, d].astype(f32), d == 1)
        y_sum = y_sum + y
        bonus = bonus + jnp.sum(rh * kd * rk, axis=-1, keepdims=True) * vh
        finals.append(s_fin)
    mean = jnp.mean(y_sum, axis=-1, keepdims=True)
    var = jnp.mean(jnp.square(y_sum - mean), axis=-1, keepdims=True)
    yn = ((y_sum - mean) * lax.rsqrt(var + LNX_EPS)).reshape(bsz, seq, D_A) * lp['rw_lnx_g'] + lp['rw_lnx_b']
    gate = jax.nn.sigmoid(g_lo) @ lp['rw_g2']
    out = (yn + bonus.reshape(bsz, seq, D_A)) * gate
    return out.astype(p.dtype), jnp.stack(finals, axis=1).astype(p.dtype)


def hyena_filters(seq, lp):
    f32 = jnp.float32
    pos = jnp.arange(seq, dtype=f32)
    t = pos[:, None] / (seq - 1)
    bands = jnp.linspace(1e-4, N_BANDS - 1, N_BANDS, dtype=f32)
    ang = (2.0 * math.pi / seq) * pos[:, None] * bands[None, :]
    z = jnp.concatenate([t, jnp.cos(ang), -jnp.sin(ang)], axis=-1)
    freq = lp['hy_freq'].astype(f32)
    h = jnp.sin(freq[0] * (z @ lp['hy_f_w1'] + lp['hy_f_b1']))
    h = jnp.sin(freq[1] * (h @ lp['hy_f_w2'] + lp['hy_f_b2']))
    h = (h @ lp['hy_f_w3'] + lp['hy_f_b3']) * jnp.exp(-t * jnp.abs(lp['hy_decay'].astype(f32)))
    return h[:, :D_B], h[:, D_B:]


def two_sided_fftconv(u, h_fwd, h_bwd):
    seq, ch = h_fwd.shape
    filt = jnp.concatenate([h_fwd, jnp.zeros((1, ch), h_fwd.dtype), h_bwd[1:][::-1]], axis=0)
    uf = jnp.fft.rfft(u, n=2 * seq, axis=1)
    ff = jnp.fft.rfft(filt, n=2 * seq, axis=0)
    return jnp.fft.irfft(uf * ff[None], n=2 * seq, axis=1)[:, :seq]


def hyena_mixer(p, lp, grid):
    u = short_conv(p, lp['hy_conv_w'], CONV_B // 2, grid) + lp['hy_conv_b']
    v, x1, x2 = jnp.split(u, 3, axis=-1)
    h_fwd, h_bwd = hyena_filters(p.shape[1], lp)
    z = (v * x1).astype(jnp.float32)
    z = two_sided_fftconv(z, h_fwd, h_bwd) + z * lp['hy_bias']
    return (x2 * z).astype(p.dtype)


def linear_scan(a, u, h0, reverse):
    def combine(e1, e2):
        a1, u1 = e1
        a2, u2 = e2
        return a1 * a2, a2 * u1 + u2
    acc_a, acc_u = lax.associative_scan(combine, (a, u), axis=1, reverse=reverse)
    return acc_a * h0[:, None, :] + acc_u


def rglru_mixer(p, lp, h0, grid):
    f32 = jnp.float32
    xm, yg = jnp.split(p, 2, axis=-1)
    xm = short_conv(xm, lp['lru_conv_w'], CONV_C // 2, grid) + lp['lru_conv_b']
    bsz, seq, _ = xm.shape
    xh = xm.reshape(bsz, seq, N_HEADS_C, HEAD_C)
    xf = xm.astype(f32)
    out, finals = 0.0, []
    for d in range(N_DIR):
        gr = jax.nn.sigmoid(jnp.einsum('blhi,hij->blhj', xh, lp['lru_wa'][d]).reshape(bsz, seq, D_C) + lp['lru_ba'][d])
        gi = jax.nn.sigmoid(jnp.einsum('blhi,hij->blhj', xh, lp['lru_wx'][d]).reshape(bsz, seq, D_C) + lp['lru_bx'][d])
        log_a = -LRU_C * gr.astype(f32) * jax.nn.softplus(-lp['lru_lam'][d].astype(f32))
        u = jnp.sqrt(-jnp.expm1(2.0 * log_a)) * gi.astype(f32) * xf
        h = linear_scan(jnp.exp(log_a), u, h0[:, d].astype(f32), d == 1)
        out = out + h
        finals.append(h[:, 0] if d == 1 else h[:, -1])
    y = out * jax.nn.gelu(yg.astype(f32))
    return y.astype(p.dtype), jnp.stack(finals, axis=1).astype(p.dtype)


def moe(h, lp):
    bsz, seq, dm = h.shape
    hf = h.reshape(-1, dm)
    n_tok = hf.shape[0]
    logits = (hf @ lp['router_w'] + lp['router_b']).astype(jnp.float32)
    top_logit, top_idx = lax.top_k(logits, TOP_K)
    gates = jax.nn.softmax(top_logit, axis=-1).astype(h.dtype)
    n_assign = n_tok * TOP_K
    flat_e = top_idx.reshape(-1)
    order = jnp.argsort(flat_e)
    e_sorted = flat_e[order]
    tok_sorted = order // TOP_K
    counts = jnp.bincount(flat_e, length=N_EXPERTS)
    padded = (counts + MOE_BLOCK - 1) // MOE_BLOCK * MOE_BLOCK
    pad_end = jnp.cumsum(padded)
    pad_start = pad_end - padded
    start = jnp.cumsum(counts) - counts
    dest = pad_start[e_sorted] + jnp.arange(n_assign) - start[e_sorted]
    n_blocks = -(-n_assign // MOE_BLOCK) + N_EXPERTS
    xs = jnp.zeros((n_blocks * MOE_BLOCK, dm), h.dtype).at[dest].set(hf[tok_sorted])
    block_e = jnp.minimum(jnp.searchsorted(pad_end, jnp.arange(n_blocks) * MOE_BLOCK, side='right'), N_EXPERTS - 1)

    def expert_block(args):
        xb, e = args
        gt = jnp.minimum(xb @ lp['exp_w_gate'][e] + lp['exp_b_gate'][e], SWIGLU_LIMIT)
        up = jnp.clip(xb @ lp['exp_w_up'][e] + lp['exp_b_up'][e], -SWIGLU_LIMIT, SWIGLU_LIMIT)
        act = (up + 1.0) * gt * jax.nn.sigmoid(SWIGLU_ALPHA * gt)
        return act @ lp['exp_w_down'][e] + lp['exp_b_down'][e]

    ys = lax.map(expert_block, (xs.reshape(n_blocks, MOE_BLOCK, dm), block_e)).reshape(-1, dm)
    contrib = ys[dest] * gates.reshape(-1)[order][:, None]
    out = jax.ops.segment_sum(contrib, tok_sorted, num_segments=n_tok)
    return out.reshape(bsz, seq, dm).astype(h.dtype)


def trunk_layer(x, mod, lp, s_rwkv0, s_lru0, grid):
    sh1, sc1, gt1, sh2, sc2, gt2 = jnp.split(mod[:, None, :], N_MOD, axis=-1)
    h = rmsnorm(x, lp['norm1_g']) * (1.0 + sc1) + sh1
    p = h @ lp['w_in']
    p_a, p_b, p_c = _split(p, (COLS_A, COLS_B, COLS_C))
    y_a, s_rw = rwkv7_mixer(p_a, lp, s_rwkv0, grid)
    y_b = hyena_mixer(p_b, lp, grid)
    y_c, s_lru = rglru_mixer(p_c, lp, s_lru0, grid)
    y = jnp.concatenate([y_a, rmsnorm(y_b, lp['hy_out_g']), rmsnorm(y_c, lp['lru_out_g'])], axis=-1) @ lp['w_out']
    x = x + gt1 * y
    h = rmsnorm(x, lp['norm2_g']) * (1.0 + sc2) + sh2
    x = x + gt2 * moe(h, lp)
    return x, s_rw, s_lru


def setup_inputs(seed: int = 0) -> dict:
    key = jax.random.key(seed)
    ks = iter(jax.random.split(key, 64))
    f32 = jnp.float32

    def nrm(shape, scale):
        return scale * jax.random.normal(next(ks), shape, f32)

    def uni(shape, lo, hi):
        return jax.random.uniform(next(ks), shape, f32, lo, hi)

    a_root = uni((DEPTH, N_DIR, D_C), 0.9, 0.999) ** (1.0 / LRU_C)
    lam = jnp.log(a_root) - jnp.log1p(-a_root)
    return {
        'x_prompt': nrm((BATCH, SEQ, D_MODEL), 1.0),
        'x_sample': nrm((DEC_BATCH, DEC_SEQ, D_MODEL), 1.0),
        'state_rwkv': nrm((DEC_BATCH, DEPTH, N_DIR, N_HEADS_A, HEAD_A, HEAD_A), 0.5),
        'state_lru': nrm((DEC_BATCH, DEPTH, N_DIR, D_C), 0.5),
        'c': nrm((DEC_BATCH, D_MODEL), 1.0),
        'c_ctx': nrm((D_MODEL,), 1.0),
        'norm1_g': 1.0 + nrm((DEPTH, D_MODEL), 0.02),
        'norm2_g': 1.0 + nrm((DEPTH, D_MODEL), 0.02),
        'final_g': 1.0 + nrm((D_MODEL,), 0.02),
        'w_mod': nrm((DEPTH, D_MODEL, N_MOD * D_MODEL), 0.5 * D_MODEL ** -0.5),
        'b_mod': nrm((DEPTH, N_MOD * D_MODEL), 0.02),
        'w_in': nrm((DEPTH, D_MODEL, D_IN), D_MODEL ** -0.5),
        'w_out': nrm((DEPTH, D_MIX, D_MODEL), D_MIX ** -0.5),
        'rw_mu': uni((DEPTH, COLS_A), 0.0, 1.0),
        'rw_w0': uni((DEPTH, N_DIR, D_A), -6.0, -1.0),
        'rw_w2': nrm((DEPTH, N_DIR, RANK_W, D_A), 0.5 * RANK_W ** -0.5),
        'rw_a0': nrm((DEPTH, N_DIR, D_A), 0.1),
        'rw_a2': nrm((DEPTH, N_DIR, RANK_A, D_A), 0.5 * RANK_A ** -0.5),
        'rw_g2': nrm((DEPTH, RANK_G, D_A), RANK_G ** -0.5),
        'rw_kk': 0.85 + nrm((DEPTH, D_A), 0.02),
        'rw_ka': 1.0 + nrm((DEPTH, D_A), 0.02),
        'rw_rk': nrm((DEPTH, N_HEADS_A, HEAD_A), 0.1),
        'rw_lnx_g': 1.0 + nrm((DEPTH, D_A), 0.02),
        'rw_lnx_b': nrm((DEPTH, D_A), 0.02),
        'hy_conv_w': nrm((DEPTH, CONV_B, COLS_B), CONV_B ** -0.5),
        'hy_conv_b': nrm((DEPTH, COLS_B), 0.02),
        'hy_f_w1': nrm((DEPTH, POS_DIM, FILTER_HIDDEN), 2.0 * POS_DIM ** -0.5),
        'hy_f_b1': nrm((DEPTH, FILTER_HIDDEN), 0.1),
        'hy_f_w2': nrm((DEPTH, FILTER_HIDDEN, FILTER_HIDDEN), FILTER_HIDDEN ** -0.5),
        'hy_f_b2': nrm((DEPTH, FILTER_HIDDEN), 0.1),
        'hy_f_w3': nrm((DEPTH, FILTER_HIDDEN, 2 * D_B), FILTER_HIDDEN ** -0.5),
        'hy_f_b3': nrm((DEPTH, 2 * D_B), 0.02),
        'hy_freq': 1.0 + nrm((DEPTH, 2, FILTER_HIDDEN), 0.1),
        'hy_decay': uni((DEPTH, 2 * D_B), 3.07, 15.35),
        'hy_bias': nrm((DEPTH, D_B), 1.0),
        'hy_out_g': 1.0 + nrm((DEPTH, D_B), 0.02),
        'lru_conv_w': nrm((DEPTH, CONV_C, D_C), CONV_C ** -0.5),
        'lru_conv_b': nrm((DEPTH, D_C), 0.02),
        'lru_wa': nrm((DEPTH, N_DIR, N_HEADS_C, HEAD_C, HEAD_C), HEAD_C ** -0.5),
        'lru_ba': nrm((DEPTH, N_DIR, D_C), 0.1),
        'lru_wx': nrm((DEPTH, N_DIR, N_HEADS_C, HEAD_C, HEAD_C), HEAD_C ** -0.5),
        'lru_bx': nrm((DEPTH, N_DIR, D_C), 0.1),
        'lru_lam': lam,
        'lru_out_g': 1.0 + nrm((DEPTH, D_C), 0.02),
        'router_w': nrm((DEPTH, D_MODEL, N_EXPERTS), D_MODEL ** -0.5),
        'router_b': nrm((DEPTH, N_EXPERTS), 0.01),
        'exp_w_gate': nrm((DEPTH, N_EXPERTS, D_MODEL, D_FF), D_MODEL ** -0.5),
        'exp_b_gate': nrm((DEPTH, N_EXPERTS, D_FF), 0.01),
        'exp_w_up': nrm((DEPTH, N_EXPERTS, D_MODEL, D_FF), D_MODEL ** -0.5),
        'exp_b_up': nrm((DEPTH, N_EXPERTS, D_FF), 0.01),
        'exp_w_down': nrm((DEPTH, N_EXPERTS, D_FF, D_MODEL), D_FF ** -0.5),
        'exp_b_down': nrm((DEPTH, N_EXPERTS, D_MODEL), 0.01),
    }


def reference(x_prompt, x_sample, state_rwkv, state_lru, c, c_ctx,
              norm1_g, norm2_g, final_g, w_mod, b_mod, w_in, w_out,
              rw_mu, rw_w0, rw_w2, rw_a0, rw_a2, rw_g2, rw_kk, rw_ka, rw_rk, rw_lnx_g, rw_lnx_b,
              hy_conv_w, hy_conv_b, hy_f_w1, hy_f_b1, hy_f_w2, hy_f_b2, hy_f_w3, hy_f_b3,
              hy_freq, hy_decay, hy_bias, hy_out_g,
              lru_conv_w, lru_conv_b, lru_wa, lru_ba, lru_wx, lru_bx, lru_lam, lru_out_g,
              router_w, router_b, exp_w_gate, exp_b_gate, exp_w_up, exp_b_up, exp_w_down, exp_b_down):
    bp = x_prompt.shape[0]
    zero_rwkv = jnp.zeros((bp, N_DIR, N_HEADS_A, HEAD_A, HEAD_A), x_prompt.dtype)
    zero_lru = jnp.zeros((bp, N_DIR, D_C), x_prompt.dtype)
    xp, xs = x_prompt, x_sample
    new_rwkv, new_lru = [], []
    for l in range(DEPTH):
        lp = {
            'norm1_g': norm1_g[l], 'norm2_g': norm2_g[l], 'w_in': w_in[l], 'w_out': w_out[l],
            'rw_mu': rw_mu[l], 'rw_w0': rw_w0[l], 'rw_w2': rw_w2[l], 'rw_a0': rw_a0[l], 'rw_a2': rw_a2[l],
            'rw_g2': rw_g2[l], 'rw_kk': rw_kk[l], 'rw_ka': rw_ka[l], 'rw_rk': rw_rk[l],
            'rw_lnx_g': rw_lnx_g[l], 'rw_lnx_b': rw_lnx_b[l],
            'hy_conv_w': hy_conv_w[l], 'hy_conv_b': hy_conv_b[l], 'hy_f_w1': hy_f_w1[l], 'hy_f_b1': hy_f_b1[l],
            'hy_f_w2': hy_f_w2[l], 'hy_f_b2': hy_f_b2[l], 'hy_f_w3': hy_f_w3[l], 'hy_f_b3': hy_f_b3[l],
            'hy_freq': hy_freq[l], 'hy_decay': hy_decay[l], 'hy_bias': hy_bias[l], 'hy_out_g': hy_out_g[l],
            'lru_conv_w': lru_conv_w[l], 'lru_conv_b': lru_conv_b[l], 'lru_wa': lru_wa[l], 'lru_ba': lru_ba[l],
            'lru_wx': lru_wx[l], 'lru_bx': lru_bx[l], 'lru_lam': lru_lam[l], 'lru_out_g': lru_out_g[l],
            'router_w': router_w[l], 'router_b': router_b[l],
            'exp_w_gate': exp_w_gate[l], 'exp_b_gate': exp_b_gate[l], 'exp_w_up': exp_w_up[l],
            'exp_b_up': exp_b_up[l], 'exp_w_down': exp_w_down[l], 'exp_b_down': exp_b_down[l],
        }
        mod_ctx = jax.nn.silu(c_ctx)[None, :] @ w_mod[l] + b_mod[l]
        mod_lat = jax.nn.silu(c) @ w_mod[l] + b_mod[l]
        xp, s_rw, s_lru = trunk_layer(xp, mod_ctx, lp, zero_rwkv, zero_lru, False)
        new_rwkv.append(s_rw)
        new_lru.append(s_lru)
        xs, _, _ = trunk_layer(xs, mod_lat, lp, state_rwkv[:, l], state_lru[:, l], True)
    y_prompt = rmsnorm(xp, final_g)
    y_sample = rmsnorm(xs, final_g)
    new_state_rwkv = jnp.stack(new_rwkv, axis=1)
    new_state_lru = jnp.stack(new_lru, axis=1)
    return (y_prompt, y_sample, new_state_rwkv, new_state_lru)
```

```python
import functools
import math

import numpy as np
import jax
import jax.numpy as jnp
from jax import lax
from jax.experimental import pallas as pl
from jax.experimental.pallas import tpu as pltpu

F32 = jnp.float32
BF16 = jnp.bfloat16

GRID_W = 64
HEAD = 64
N_HEADS_A = 6
D_A = N_HEADS_A * HEAD
D_B = 256
D_C = 384
COLS_A = 3 * D_A + 384
COLS_B = 3 * D_B
COLS_C = 2 * D_C
N_EXPERTS = 32
TOP_K = 4
N_BANDS = 16
LRU_C = 8.0
SWIGLU_LIMIT = 7.0
SWIGLU_ALPHA = 1.702
EPS = 1e-6
LNX_EPS = 64e-5

LANES = 128
SEQ_BLOCK = 256
TOK_TILE = 256
CHUNK = 64
ROUTE_TILE = 512
EXPERT_ROWS = 512
VMEM_LIMIT = 56 * 1024 * 1024

NN = (((1,), (0,)), ((), ()))
NT = (((1,), (1,)), ((), ()))
TN = (((0,), (0,)), ((), ()))


def _dg(a, b, dims=NN):
    return lax.dot_general(a, b, dims, preferred_element_type=F32)


def _dot1(a, b, dims=NN):
    return _dg(a.astype(BF16), b.astype(BF16), dims)


def _split(x):
    hi = x.astype(BF16)
    return hi, (x - hi.astype(F32)).astype(BF16)


def _dot3(a, b, dims=NN):
    ah, al = _split(a)
    bh, bl = _split(b)
    return _dg(ah, bh, dims) + (_dg(ah, bl, dims) + _dg(al, bh, dims))


def _dot_exact_rhs(a, b_bf16, dims=NN):
    a1 = a.astype(BF16)
    r1 = a - a1.astype(F32)
    a2 = r1.astype(BF16)
    a3 = (r1 - a2.astype(F32)).astype(BF16)
    return _dg(a1, b_bf16, dims) + (_dg(a2, b_bf16, dims) + _dg(a3, b_bf16, dims))


def _dot_exact_lhs(a_bf16, b, dims=NN):
    b1 = b.astype(BF16)
    r1 = b - b1.astype(F32)
    b2 = r1.astype(BF16)
    b3 = (r1 - b2.astype(F32)).astype(BF16)
    return _dg(a_bf16, b1, dims) + (_dg(a_bf16, b2, dims) + _dg(a_bf16, b3, dims))


def _iota(shape, axis):
    return lax.broadcasted_iota(jnp.int32, shape, axis)


def _head_ones(n):
    return ((_iota((n, n), 0) // HEAD) == (_iota((n, n), 1) // HEAD)).astype(BF16)


def _sigmoid(x):
    return 1.0 / (1.0 + jnp.exp(-x))


def _params(sem):
    return pltpu.CompilerParams(dimension_semantics=sem, vmem_limit_bytes=VMEM_LIMIT)


def _shifted(u, shift, pos, seg):
    n = u.shape[0]
    rolled = pltpu.roll(u, (-shift) % n, 0)
    ok = (pos + shift >= 0) & (pos + shift < seg)
    return jnp.where(ok, rolled, 0.0)


def _mod_kernel(c_ref, w_ref, b_ref, o_ref):
    c = c_ref[...]
    s = c * _sigmoid(c)
    o_ref[0] = _dot3(s, w_ref[0]) + b_ref[0]


def _modulation(cc, w_mod, b_mod):
    depth, d, n = w_mod.shape
    r = cc.shape[0]
    tn = 1024
    return pl.pallas_call(
        _mod_kernel,
        grid=(depth, n // tn),
        in_specs=[
            pl.BlockSpec((r, d), lambda l, j: (0, 0)),
            pl.BlockSpec((1, d, tn), lambda l, j: (l, 0, j)),
            pl.BlockSpec((1, 1, tn), lambda l, j: (l, 0, j)),
        ],
        out_specs=pl.BlockSpec((1, r, tn), lambda l, j: (l, 0, j)),
        out_shape=jax.ShapeDtypeStruct((depth, r, n), F32),
        compiler_params=_params(("arbitrary", "arbitrary")),
        name="modulation",
    )(cc, w_mod, b_mod.reshape(depth, 1, n))


def _rms(x):
    return x * lax.rsqrt(jnp.mean(x * x, axis=-1, keepdims=True) + EPS)


def _inproj_kernel(x_ref, mod_ref, g_ref, w_ref, o_ref):
    h = _rms(x_ref[...]) * g_ref[...]
    h = h * (1.0 + mod_ref[0, 1:2, :]) + mod_ref[0, 0:1, :]
    o_ref[...] = _dg(h.astype(BF16), w_ref[...])


def _in_proj(x, mod, g, w_bf16, mod_row):
    n_tok, d = x.shape
    n = w_bf16.shape[1]
    return pl.pallas_call(
        _inproj_kernel,
        grid=(n_tok // TOK_TILE,),
        in_specs=[
            pl.BlockSpec((TOK_TILE, d), lambda i: (i, 0)),
            pl.BlockSpec((1, 6, d), lambda i: (mod_row(i), 0, 0)),
            pl.BlockSpec((1, d), lambda i: (0, 0)),
            pl.BlockSpec((d, n), lambda i: (0, 0)),
        ],
        out_specs=pl.BlockSpec((TOK_TILE, n), lambda i: (i, 0)),
        out_shape=jax.ShapeDtypeStruct((n_tok, n), F32),
        compiler_params=_params(("arbitrary",)),
        name="in_proj",
    )(x, mod, g.reshape(1, d), w_bf16)


def _rw_pre_kernel(seg, p_ref, mu_ref, w0_ref, w2_ref, a0_ref, a2_ref, g2_ref, kkw_ref, ka_ref, rk_ref,
                   r_ref, v_ref, kk_ref, kd_ref, bb_ref, lw_ref, bonus_ref, gate_ref):
    p = p_ref[...]
    n = p.shape[0]
    pos = _iota((n, 1), 0) % seg
    ps = p + (0.5 * _shifted(p, -1, pos, seg) + 0.5 * _shifted(p, 1, pos, seg) - p) * mu_ref[...]
    r = ps[:, 0:D_A]
    k = ps[:, D_A:2 * D_A]
    v = ps[:, 2 * D_A:3 * D_A]
    lo = ps[:, 3 * D_A:3 * D_A + 256]
    g_lo = ps[:, 3 * D_A + 256:3 * D_A + 384]
    ones = _head_ones(D_A)
    kk = k * kkw_ref[...]
    kk = kk * lax.rsqrt(_dot_exact_rhs(kk * kk, ones) + 1e-12)
    tanh_wl = jnp.tanh(lo[:, 0:128])
    al = lo[:, 128:256]
    bonus = jnp.zeros_like(r)
    for d in range(2):
        x = w0_ref[d] + _dot3(tanh_wl, w2_ref[d])
        lw_ref[d] = -_sigmoid(x) * math.exp(-0.5)
        asig = _sigmoid(a0_ref[d] + _dot3(al, a2_ref[d]))
        kd = k * (1.0 + (asig - 1.0) * ka_ref[...])
        kd_ref[d] = kd
        bb_ref[d] = kk * asig
        bonus = bonus + _dot_exact_rhs(r * kd * rk_ref[...], ones)
    r_ref[...] = r
    v_ref[...] = v
    kk_ref[...] = kk
    bonus_ref[...] = bonus * v
    gate_ref[...] = _dot3(_sigmoid(g_lo), g2_ref[...])


def _rw_pre(p, lw, off, bsz, nb, seg):
    n = bsz * nb * SEQ_BLOCK
    tok = lambda b, i: (off + b * nb + i, 0)
    out_tok = lambda b, i: (b * nb + i, 0)
    out_tok2 = lambda b, i: (0, b * nb + i, 0)
    full = lambda *shape: pl.BlockSpec(shape, lambda b, i: (0,) * len(shape))
    one = jax.ShapeDtypeStruct((n, D_A), F32)
    two = jax.ShapeDtypeStruct((2, n, D_A), F32)
    s1 = pl.BlockSpec((SEQ_BLOCK, D_A), out_tok)
    s2 = pl.BlockSpec((2, SEQ_BLOCK, D_A), out_tok2)
    return pl.pallas_call(
        functools.partial(_rw_pre_kernel, seg),
        grid=(bsz, nb),
        in_specs=[
            pl.BlockSpec((SEQ_BLOCK, COLS_A), tok),
            full(1, COLS_A), full(2, 1, D_A), full(2, 128, D_A), full(2, 1, D_A), full(2, 128, D_A),
            full(128, D_A), full(1, D_A), full(1, D_A), full(1, D_A),
        ],
        out_specs=[s1, s1, s1, s2, s2, s2, s1, s1],
        out_shape=[one, one, one, two, two, two, one, one],
        compiler_params=_params(("arbitrary", "arbitrary")),
        name="rwkv_pre",
    )(p, lw["mu"], lw["w0"], lw["w2p"], lw["a0"], lw["a2p"], lw["g2"], lw["kk"], lw["ka"], lw["rk"])


def _unit_tri_inverse(nmat, eye, r2, c2):
    b16 = (r2 // 16) == (c2 // 16)
    b32 = (r2 // 32) == (c2 // 32)
    nd = jnp.where(b16, nmat, 0.0)
    x = eye + nd
    pw = nd
    for _ in range(3):
        pw = _dot3(pw, pw)
        x = x + _dot3(x, pw)
    off1 = jnp.where(b32 & ~b16, nmat, 0.0)
    x = x + _dot3(_dot3(x, off1), x)
    off2 = jnp.where(~b32, nmat, 0.0)
    x = x + _dot3(_dot3(x, off2), x)
    return x


def _rw_scan_kernel(rev, nb, r_ref, v_ref, kk_ref, kd_ref, bb_ref, lw_ref, s0_ref, y_ref, sfin_ref, s_scr):
    i = pl.program_id(1)

    @pl.when(i == 0)
    def _():
        s_scr[...] = s0_ref[0]

    c = CHUNK
    nsub = r_ref.shape[0] // c
    row = _iota((c, c), 0)
    col = _iota((c, c), 1)
    tri = ((col >= row) if rev else (col <= row)).astype(BF16)
    r2 = _iota((2 * c, 2 * c), 0)
    c2 = _iota((2 * c, 2 * c), 1)
    hmask = (r2 // c) == (c2 // c)
    tr = r2 % c
    tc = c2 % c
    m_strict = hmask & ((tr < tc) if rev else (tr > tc))
    m_incl = hmask & ((tr <= tc) if rev else (tr >= tc))
    eye = (r2 == c2).astype(F32)
    lane_lo = _iota((c, 2 * c), 1) < c

    def stack_masked(x):
        return jnp.concatenate([jnp.where(lane_lo, x, 0.0), jnp.where(lane_lo, 0.0, x)], axis=0)

    def body(j, carry):
        sub = (nsub - 1 - j) if rev else j
        sl = pl.ds(pl.multiple_of(sub * c, c), c)
        r = r_ref[sl, :]
        v = v_ref[sl, :]
        kk = kk_ref[sl, :]
        kd = kd_ref[sl, :]
        bb = bb_ref[sl, :]
        lw = lw_ref[sl, :]
        cl = _dot_exact_lhs(tri, lw)
        tot = cl[0:1, :] if rev else cl[c - 1:c, :]
        e_in = jnp.exp(cl)
        e_out = jnp.exp(-cl)
        e_rest = jnp.exp(tot - cl)
        rt = r * e_in
        at = -kk * jnp.exp(cl - lw)
        bt = bb * e_out
        kt = kd * e_out
        btp = bb * e_rest
        ktp = kd * e_rest
        ptot = jnp.exp(tot)
        for pr in range(N_HEADS_A // 2):
            ls = slice(2 * c * pr, 2 * c * pr + 2 * c)
            a_s = stack_masked(at[:, ls])
            r_s = stack_masked(rt[:, ls])
            b2 = jnp.concatenate([bt[:, ls]] * 2, axis=0)
            k2 = jnp.concatenate([kt[:, ls]] * 2, axis=0)
            v2 = stack_masked(v[:, ls])
            g = _dot3(jnp.concatenate([a_s, r_s], axis=0), jnp.concatenate([b2, k2], axis=0), NT)
            n_ab = jnp.where(m_strict, g[0:2 * c, 0:2 * c], 0.0)
            a_ak = jnp.where(m_strict, g[0:2 * c, 2 * c:4 * c], 0.0)
            a_rb = jnp.where(m_incl, g[2 * c:4 * c, 0:2 * c], 0.0)
            a_rk = jnp.where(m_incl, g[2 * c:4 * c, 2 * c:4 * c], 0.0)
            tinv = _unit_tri_inverse(n_ab, eye, r2, c2)
            s = s_scr[pr]
            u = _dot3(tinv, _dot3(a_s, s, NT) + _dot3(a_ak, v2))
            ys = _dot3(r_s, s, NT) + _dot3(a_rb, u) + _dot3(a_rk, v2)
            y_ref[sl, ls] = ys[0:c, :] + ys[c:2 * c, :]
            b2p = jnp.concatenate([btp[:, ls]] * 2, axis=0)
            k2p = jnp.concatenate([ktp[:, ls]] * 2, axis=0)
            upd = _dot3(jnp.concatenate([u, v2], axis=0), jnp.concatenate([b2p, k2p], axis=0), TN)
            s_scr[pr] = s * ptot[:, ls] + jnp.where(hmask, upd, 0.0)
        return carry

    lax.fori_loop(0, nsub, body, 0)

    @pl.when(i == nb - 1)
    def _():
        sfin_ref[0] = s_scr[...]


def _rw_scan(pre, s0_pairs, d, bsz, nb):
    r, v, kk, kd, bb, lw = pre
    rev = d == 1
    n = bsz * nb * SEQ_BLOCK
    blk = (lambda i: nb - 1 - i) if rev else (lambda i: i)
    tok = lambda b, i: (b * nb + blk(i), 0)
    tok2 = lambda b, i: (d, b * nb + blk(i), 0)
    st = lambda b, i: (b, 0, 0, 0)
    s1 = pl.BlockSpec((SEQ_BLOCK, D_A), tok)
    s2 = pl.BlockSpec((None, SEQ_BLOCK, D_A), tok2)
    sst = pl.BlockSpec((1, 3, 2 * HEAD, 2 * HEAD), st)
    return pl.pallas_call(
        functools.partial(_rw_scan_kernel, rev, nb),
        grid=(bsz, nb),
        in_specs=[s1, s1, s1, s2, s2, s2, sst],
        out_specs=[s1, sst],
        out_shape=[jax.ShapeDtypeStruct((n, D_A), F32),
                   jax.ShapeDtypeStruct((bsz, 3, 2 * HEAD, 2 * HEAD), F32)],
        scratch_shapes=[pltpu.VMEM((3, 2 * HEAD, 2 * HEAD), F32)],
        compiler_params=_params(("arbitrary", "arbitrary")),
        name="rwkv_scan_bwd" if rev else "rwkv_scan_fwd",
    )(r, v, kk, kd, bb, lw, s0_pairs)


def _to_pairs(s):
    b = s.shape[0]
    s = s.reshape(b, 3, 2, HEAD, HEAD)
    z = jnp.zeros_like(s[:, :, 0])
    top = jnp.concatenate([s[:, :, 0], z], axis=-1)
    bot = jnp.concatenate([z, s[:, :, 1]], axis=-1)
    return jnp.concatenate([top, bot], axis=-2)


def _from_pairs(sp):
    b = sp.shape[0]
    return jnp.stack([sp[:, :, :HEAD, :HEAD], sp[:, :, HEAD:, HEAD:]], axis=2).reshape(b, 6, HEAD, HEAD)


def _lru_kernel(rev, nb, seg, p_ref, cw_ref, cb_ref, wg_ref, bg_ref, lam_ref, h0_ref, h_ref, hfin_ref, hc_scr):
    i = pl.program_id(1)

    @pl.when(i == 0)
    def _():
        hc_scr[...] = h0_ref[0]

    u_in = p_ref[...]
    n = u_in.shape[0]
    t = _iota((n, 1), 0)
    pos = t % seg
    cw = cw_ref[...]
    xm = (cw[0:1] * _shifted(u_in, -2, pos, seg) + cw[1:2] * _shifted(u_in, -1, pos, seg)
          + cw[2:3] * u_in + cw[3:4] * _shifted(u_in, 1, pos, seg)) + cb_ref[...]
    g = _dot3(xm, wg_ref[...]) + bg_ref[...]
    gr = _sigmoid(g[:, 0:D_C])
    gi = _sigmoid(g[:, D_C:2 * D_C])
    lam = lam_ref[...]
    softplus_neg_lam = jnp.maximum(-lam, 0.0) + jnp.log1p(jnp.exp(-jnp.abs(lam)))
    log_a = -LRU_C * gr * softplus_neg_lam
    a = jnp.exp(log_a)
    u = jnp.sqrt(-jnp.tanh(log_a) * (a * a + 1.0)) * gi * xm
    s = 1
    while s < n:
        if rev:
            ok = t < n - s
            a_sh = jnp.where(ok, pltpu.roll(a, n - s, 0), 1.0)
            u_sh = jnp.where(ok, pltpu.roll(u, n - s, 0), 0.0)
        else:
            ok = t >= s
            a_sh = jnp.where(ok, pltpu.roll(a, s, 0), 1.0)
            u_sh = jnp.where(ok, pltpu.roll(u, s, 0), 0.0)
        u = a * u_sh + u
        a = a * a_sh
        s *= 2
    h = a * hc_scr[...] + u
    h_ref[...] = h
    hc_scr[...] = h[0:1, :] if rev else h[n - 1:n, :]

    @pl.when(i == nb - 1)
    def _():
        hfin_ref[0] = hc_scr[...]


def _lru(p, lw, h0, d, off, bsz, nb, seg):
    rev = d == 1
    n = bsz * nb * SEQ_BLOCK
    blk = (lambda i: nb - 1 - i) if rev else (lambda i: i)
    pcol = (COLS_A + COLS_B) // D_C
    full = lambda *shape: pl.BlockSpec(shape, lambda b, i: (0,) * len(shape))
    dsel = lambda *shape: pl.BlockSpec((None,) + shape, lambda b, i: (d,) + (0,) * len(shape))
    return pl.pallas_call(
        functools.partial(_lru_kernel, rev, nb, seg),
        grid=(bsz, nb),
        in_specs=[
            pl.BlockSpec((SEQ_BLOCK, D_C), lambda b, i: (off + b * nb + blk(i), pcol)),
            full(4, D_C), full(1, D_C), dsel(D_C, 2 * D_C), dsel(1, 2 * D_C), dsel(1, D_C),
            pl.BlockSpec((1, None, 1, D_C), lambda b, i: (b, d, 0, 0)),
        ],
        out_specs=[pl.BlockSpec((SEQ_BLOCK, D_C), lambda b, i: (b * nb + blk(i), 0)),
                   pl.BlockSpec((1, 1, D_C), lambda b, i: (b, 0, 0))],
        out_shape=[jax.ShapeDtypeStruct((n, D_C), F32), jax.ShapeDtypeStruct((bsz, 1, D_C), F32)],
        scratch_shapes=[pltpu.VMEM((1, D_C), F32)],
        compiler_params=_params(("arbitrary", "arbitrary")),
        name="rglru_bwd" if rev else "rglru_fwd",
    )(p, lw["cw"], lw["cb"], lw["wg"], lw["bg"], lw["lam"], h0)


def _rw_post_kernel(yf_ref, yb_ref, bonus_ref, gate_ref, g_ref, b_ref, o_ref):
    y = yf_ref[...] + yb_ref[...]
    ones = _head_ones(D_A)
    mean = _dot_exact_rhs(y, ones) * (1.0 / HEAD)
    yc = y - mean
    var = _dot_exact_rhs(yc * yc, ones) * (1.0 / HEAD)
    yn = yc * lax.rsqrt(var + LNX_EPS) * g_ref[...] + b_ref[...]
    o_ref[...] = (yn + bonus_ref[...]) * gate_ref[...]


def _rw_post(yf, yb, bonus, gate, lw):
    n = yf.shape[0]
    s1 = pl.BlockSpec((TOK_TILE, D_A), lambda i: (i, 0))
    w1 = pl.BlockSpec((1, D_A), lambda i: (0, 0))
    return pl.pallas_call(
        _rw_post_kernel,
        grid=(n // TOK_TILE,),
        in_specs=[s1, s1, s1, s1, w1, w1],
        out_specs=s1,
        out_shape=jax.ShapeDtypeStruct((n, D_A), F32),
        compiler_params=_params(("arbitrary",)),
        name="rwkv_post",
    )(yf, yb, bonus, gate, lw["lnx_g"], lw["lnx_b"])


def _rw_weights(lp):
    def pad_dir(w, slot):
        z = jnp.zeros((2, 128, D_A), F32)
        z = z.at[0, 0:64].set(w[0])
        return z.at[1, 64:128].set(w[1])
    return {
        "mu": lp["rw_mu"].reshape(1, COLS_A),
        "w0": lp["rw_w0"].reshape(2, 1, D_A), "w2p": pad_dir(lp["rw_w2"], 0),
        "a0": lp["rw_a0"].reshape(2, 1, D_A), "a2p": pad_dir(lp["rw_a2"], 1),
        "g2": lp["rw_g2"], "kk": lp["rw_kk"].reshape(1, D_A), "ka": lp["rw_ka"].reshape(1, D_A),
        "rk": lp["rw_rk"].reshape(1, D_A),
        "lnx_g": lp["rw_lnx_g"].reshape(1, D_A), "lnx_b": lp["rw_lnx_b"].reshape(1, D_A),
    }


def _rwkv_mixer(p, lw, s0, off, bsz, nb, seg):
    r, v, kk, kd, bb, lwd, bonus, gate = _rw_pre(p, lw, off, bsz, nb, seg)
    ys, fins = [], []
    for d in range(2):
        y, sfin = _rw_scan((r, v, kk, kd, bb, lwd), _to_pairs(s0[:, d]), d, bsz, nb)
        ys.append(y)
        fins.append(_from_pairs(sfin))
    return _rw_post(ys[0], ys[1], bonus, gate, lw), jnp.stack(fins, axis=1)


def _lru_weights(lp):
    def block_diag(w):
        eye = jnp.eye(N_HEADS_A, dtype=F32)
        return jnp.einsum("hij,hg->higj", w, eye).reshape(D_C, D_C)
    wg = jnp.stack([jnp.concatenate([block_diag(lp["lru_wa"][d]), block_diag(lp["lru_wx"][d])], axis=1)
                    for d in range(2)])
    bg = jnp.stack([jnp.concatenate([lp["lru_ba"][d], lp["lru_bx"][d]])[None] for d in range(2)])
    return {"cw": lp["lru_conv_w"], "cb": lp["lru_conv_b"].reshape(1, D_C), "wg": wg, "bg": bg,
            "lam": lp["lru_lam"].reshape(2, 1, D_C)}


def _lru_mixer(p, lw, h0, off, bsz, nb, seg):
    h0 = h0.reshape(bsz, 2, 1, D_C)
    hf, ff = _lru(p, lw, h0, 0, off, bsz, nb, seg)
    hb, fb = _lru(p, lw, h0, 1, off, bsz, nb, seg)
    return hf, hb, jnp.concatenate([ff, fb], axis=1)


def _dft_tables(seq):
    n = 2 * seq
    n2 = 128 if n >= 4096 else 16
    n1 = n // n2
    n1h = n1 // 2
    i1 = np.arange(n1h)[None, None, :]
    k1 = np.arange(n1)[None, :, None]
    i2 = np.arange(n2)[:, None, None]
    ph = 2.0 * np.pi * (i1 * k1 / n1 + i2 * k1 / n)
    f1 = np.concatenate([np.cos(ph), -np.sin(ph)], axis=1)
    g3 = np.concatenate([np.cos(ph), -np.sin(ph)], axis=1).transpose(0, 2, 1) / n
    a = 2.0 * np.pi * np.outer(np.arange(n2), np.arange(n2)) / n2
    f2r, f2i = np.cos(a), -np.sin(a)
    f2 = np.block([[f2r, -f2i], [f2i, f2r]])
    f2c = np.block([[f2r, f2i], [-f2i, f2r]])
    as32 = lambda t: jnp.asarray(t, F32)
    return n1, n2, as32(f1), as32(f2), as32(f2c), as32(g3)


def _dft_stage1(src_ref, dst_ref, f1_ref, n1, n2):
    n1h = n1 // 2

    def body(i, carry):
        xs = src_ref[pl.ds(i, n1h, stride=n2), :]
        a = _dot3(f1_ref[i], xs)
        dst_ref[pl.ds(i, n1, stride=2 * n2), :] = a[0:n1]
        dst_ref[pl.ds(n2 + i, n1, stride=2 * n2), :] = a[n1:2 * n1]
        return carry

    lax.fori_loop(0, n2, body, 0)


def _hy_filter_kernel(n1, n2, z_ref, w1_ref, b1_ref, w2_ref, b2_ref, w3f_ref, w3b_ref, b3f_ref, b3b_ref,
                      dcf_ref, dcb_ref, freq_ref, f1_ref, f2_ref, o_ref, hf_scr, hb_scr, xf_scr, xb_scr):
    z = z_ref[...]
    h = jnp.sin(freq_ref[0:1, :] * (_dot3(z, w1_ref[...]) + b1_ref[...]))
    h = jnp.sin(freq_ref[1:2, :] * (_dot3(h, w2_ref[...]) + b2_ref[...]))
    t = z[:, 0:1]
    hf_scr[...] = (_dot3(h, w3f_ref[...]) + b3f_ref[...]) * jnp.exp(-t * jnp.abs(dcf_ref[...]))
    hb = (_dot3(h, w3b_ref[...]) + b3b_ref[...]) * jnp.exp(-t * jnp.abs(dcb_ref[...]))
    hb_scr[...] = jnp.where(_iota(hb.shape, 0) == 0, 0.0, hb)
    _dft_stage1(hf_scr, xf_scr, f1_ref, n1, n2)
    _dft_stage1(hb_scr, xb_scr, f1_ref, n1, n2)
    f2 = f2_ref[...]

    def body(k, carry):
        sl = pl.ds(pl.multiple_of(k * 2 * n2, 2 * n2), 2 * n2)
        xf = _dot3(f2, xf_scr[sl, :])
        xb = _dot3(f2, xb_scr[sl, :])
        o_ref[sl, :] = jnp.concatenate([xf[0:n2] + xb[0:n2], xf[n2:2 * n2] - xb[n2:2 * n2]], axis=0)
        return carry

    lax.fori_loop(0, n1, body, 0)


def _hy_filter(lw, seq, tables):
    n1, n2, f1, f2, _, _ = tables
    ct = LANES
    nct = D_B // ct
    full = lambda a: pl.BlockSpec(a.shape, lambda j: (0,) * a.ndim)
    col = lambda rows, shift: pl.BlockSpec((rows, ct), lambda j: (0, shift + j))
    pos = jnp.arange(seq, dtype=F32)
    t = pos[:, None] / (seq - 1)
    bands = jnp.linspace(1e-4, N_BANDS - 1, N_BANDS, dtype=F32)
    ang = (2.0 * math.pi / seq) * pos[:, None] * bands[None, :]
    z = jnp.concatenate([t, jnp.cos(ang), -jnp.sin(ang), jnp.zeros((seq, 64 - 1 - 2 * N_BANDS), F32)], axis=-1)
    args = [z, lw["w1p"], lw["b1"], lw["w2"], lw["b2"]]
    return pl.pallas_call(
        functools.partial(_hy_filter_kernel, n1, n2),
        grid=(nct,),
        in_specs=[full(a) for a in args] + [
            col(64, 0), col(64, nct), col(1, 0), col(1, nct), col(1, 0), col(1, nct),
            full(lw["freq"]), full(f1), full(f2)],
        out_specs=pl.BlockSpec((4 * seq, ct), lambda j: (0, j)),
        out_shape=jax.ShapeDtypeStruct((4 * seq, D_B), F32),
        scratch_shapes=[pltpu.VMEM((seq, ct), F32), pltpu.VMEM((seq, ct), F32),
                        pltpu.VMEM((4 * seq, ct), F32), pltpu.VMEM((4 * seq, ct), F32)],
        compiler_params=_params(("arbitrary",)),
        name="hyena_filter",
    )(*args, lw["w3"], lw["w3"], lw["b3"], lw["b3"], lw["decay"], lw["decay"], lw["freq"], f1, f2)


def _hyena_kernel(n1, n2, seg, pv_ref, px1_ref, px2_ref, cwv_ref, cwx1_ref, cwx2_ref, cbv_ref, cbx1_ref, cbx2_ref,
                  bias_ref, h_ref, f1_ref, f2_ref, f2c_ref, g3_ref, o_ref, z_scr, x_scr, y_scr):
    seq = pv_ref.shape[0]
    pos = _iota((seq, 1), 0) % seg

    def conv(p_ref, w_ref, b_ref):
        p = p_ref[...]
        w = w_ref[...]
        return w[0:1] * _shifted(p, -1, pos, seg) + w[1:2] * p + w[2:3] * _shifted(p, 1, pos, seg) + b_ref[...]

    z_scr[...] = conv(pv_ref, cwv_ref, cbv_ref) * conv(px1_ref, cwx1_ref, cbx1_ref)
    _dft_stage1(z_scr, x_scr, f1_ref, n1, n2)
    f2 = f2_ref[...]
    f2c = f2c_ref[...]

    def mid(k, carry):
        sl = pl.ds(pl.multiple_of(k * 2 * n2, 2 * n2), 2 * n2)
        x = _dot3(f2, x_scr[sl, :])
        hh = h_ref[sl, :]
        xr, xi = x[0:n2], x[n2:2 * n2]
        hr, hi = hh[0:n2], hh[n2:2 * n2]
        y = jnp.concatenate([xr * hr - xi * hi, xr * hi + xi * hr], axis=0)
        x_scr[sl, :] = _dot3(f2c, y)
        return carry

    lax.fori_loop(0, n1, mid, 0)
    n1h = n1 // 2

    def last(i, carry):
        d = jnp.concatenate([x_scr[pl.ds(i, n1, stride=2 * n2), :], x_scr[pl.ds(n2 + i, n1, stride=2 * n2), :]], axis=0)
        y_scr[pl.ds(i, n1h, stride=n2), :] = _dot3(g3_ref[i], d)
        return carry

    lax.fori_loop(0, n2, last, 0)
    o_ref[...] = conv(px2_ref, cwx2_ref, cbx2_ref) * (y_scr[...] + z_scr[...] * bias_ref[...])


def _hyena(p, lw, hspec, tables, off_seq, bsz, seq, seg):
    n1, n2, f1, f2, f2c, g3 = tables
    ct = LANES
    nct = D_B // ct
    c0 = COLS_A // ct
    pcol = lambda g: pl.BlockSpec((seq, ct), lambda j, b: (off_seq + b, c0 + g * nct + j))
    wcol = lambda rows, g: pl.BlockSpec((rows, ct), lambda j, b: (0, g * nct + j))
    once = lambda a: pl.BlockSpec(a.shape, lambda j, b: (0,) * a.ndim)
    return pl.pallas_call(
        functools.partial(_hyena_kernel, n1, n2, seg),
        grid=(nct, bsz),
        in_specs=[pcol(0), pcol(1), pcol(2), wcol(3, 0), wcol(3, 1), wcol(3, 2), wcol(1, 0), wcol(1, 1), wcol(1, 2),
                  wcol(1, 0), pl.BlockSpec((4 * seq, ct), lambda j, b: (0, j)),
                  once(f1), once(f2), once(f2c), once(g3)],
        out_specs=pl.BlockSpec((seq, ct), lambda j, b: (b, j)),
        out_shape=jax.ShapeDtypeStruct((bsz * seq, D_B), F32),
        scratch_shapes=[pltpu.VMEM((seq, ct), F32), pltpu.VMEM((4 * seq, ct), F32), pltpu.VMEM((seq, ct), F32)],
        compiler_params=_params(("arbitrary", "arbitrary")),
        name="hyena",
    )(p, p, p, lw["cw"], lw["cw"], lw["cw"], lw["cb"], lw["cb"], lw["cb"], lw["bias"], hspec, f1, f2, f2c, g3)


def _hy_weights(lp):
    pos_dim = 1 + 2 * N_BANDS
    return {
        "w1p": jnp.zeros((64, 64), F32).at[0:pos_dim].set(lp["hy_f_w1"]), "b1": lp["hy_f_b1"].reshape(1, 64),
        "w2": lp["hy_f_w2"], "b2": lp["hy_f_b2"].reshape(1, 64),
        "w3": lp["hy_f_w3"], "b3": lp["hy_f_b3"].reshape(1, 2 * D_B), "decay": lp["hy_decay"].reshape(1, 2 * D_B),
        "freq": lp["hy_freq"], "cw": lp["hy_conv_w"], "cb": lp["hy_conv_b"].reshape(1, COLS_B),
        "bias": lp["hy_bias"].reshape(1, D_B),
    }


def _hyena_mixer(p, lw, off_seq, bsz, seq, seg):
    tables = _dft_tables(seq)
    hspec = _hy_filter(lw, seq, tables)
    return _hyena(p, lw, hspec, tables, off_seq, bsz, seq, seg)


def _gelu_tanh(x):
    return 0.5 * x * (1.0 + jnp.tanh(math.sqrt(2.0 / math.pi) * (x + 0.044715 * (x * x * x))))


def _outproj_kernel(ya_ref, yb_ref, hf_ref, hb_ref, yg_ref, x_ref, mod_ref, w_ref, gb_ref, gc_ref, g2_ref,
                    rw_ref, rb_ref, x1_ref, h2_ref, lg_ref):
    ybn = _rms(yb_ref[...]) * gb_ref[...]
    ycn = _rms((hf_ref[...] + hb_ref[...]) * _gelu_tanh(yg_ref[...])) * gc_ref[...]
    y = (_dot1(ya_ref[...], w_ref[0:D_A, :]) + _dot1(ybn, w_ref[D_A:D_A + D_B, :])
         + _dot1(ycn, w_ref[D_A + D_B:D_A + D_B + D_C, :]))
    x1 = x_ref[...] + mod_ref[0, 2:3, :] * y
    x1_ref[...] = x1
    h2 = _rms(x1) * g2_ref[...] * (1.0 + mod_ref[0, 4:5, :]) + mod_ref[0, 3:4, :]
    h2_ref[...] = h2
    lg_ref[...] = _dot3(rw_ref[...], h2, NT) + rb_ref[...]


def _out_proj(ya, yb, hf, hb, p, x, mod, lp, mod_row):
    n_tok, d = x.shape
    tile = lambda w: pl.BlockSpec((TOK_TILE, w), lambda i: (i, 0))
    full = lambda *shape: pl.BlockSpec(shape, lambda i: (0,) * len(shape))
    yg_col = (COLS_A + COLS_B + D_C) // D_C
    return pl.pallas_call(
        _outproj_kernel,
        grid=(n_tok // TOK_TILE,),
        in_specs=[tile(D_A), tile(D_B), tile(D_C), tile(D_C),
                  pl.BlockSpec((TOK_TILE, D_C), lambda i: (i, yg_col)), tile(d),
                  pl.BlockSpec((1, 6, d), lambda i: (mod_row(i), 0, 0)),
                  full(d, d), full(1, D_B), full(1, D_C), full(1, d), full(N_EXPERTS, d), full(N_EXPERTS, 1)],
        out_specs=[tile(d), tile(d), pl.BlockSpec((N_EXPERTS, TOK_TILE), lambda i: (0, i))],
        out_shape=[jax.ShapeDtypeStruct((n_tok, d), F32), jax.ShapeDtypeStruct((n_tok, d), F32),
                   jax.ShapeDtypeStruct((N_EXPERTS, n_tok), F32)],
        compiler_params=_params(("arbitrary",)),
        name="out_proj",
    )(ya, yb, hf, hb, p, x, mod, lp["w_out"].astype(BF16), lp["hy_out_g"].reshape(1, D_B),
      lp["lru_out_g"].reshape(1, D_C), lp["norm2_g"].reshape(1, d), lp["router_w"].T,
      lp["router_b"].reshape(N_EXPERTS, 1))


def _route_kernel(lg_ref, eidx_ref, gate_ref, rank_ref, cnt_ref, run_scr):
    i = pl.program_id(0)

    @pl.when(i == 0)
    def _():
        run_scr[...] = jnp.zeros_like(run_scr)

    l = lg_ref[...]
    ne, t = l.shape
    rowi = _iota((ne, t), 0)
    row8 = _iota((8, t), 0)
    tops, hots = [], []
    oh_all = jnp.zeros((ne, t), F32)
    eidx = jnp.zeros((8, t), jnp.int32)
    for j in range(TOP_K):
        m = jnp.max(l, axis=0, keepdims=True)
        idx = jnp.min(jnp.where(l == m, rowi, ne), axis=0, keepdims=True)
        oh = rowi == idx
        tops.append(m)
        hots.append(oh)
        oh_all = oh_all + oh.astype(F32)
        eidx = jnp.where(row8 == j, idx, eidx)
        l = jnp.where(oh, -jnp.inf, l)
    es = [jnp.exp(m - tops[0]) for m in tops]
    denom = es[0] + es[1] + es[2] + es[3]
    before = (_iota((t, t), 0) < _iota((t, t), 1)).astype(BF16)
    cum = _dg(oh_all.astype(BF16), before) + run_scr[...]
    gates = jnp.zeros((8, t), F32)
    rank = jnp.zeros((8, t), jnp.int32)
    for j in range(TOP_K):
        gates = jnp.where(row8 == j, es[j] / denom, gates)
        rj = jnp.sum(jnp.where(hots[j], cum, 0.0), axis=0, keepdims=True)
        rank = jnp.where(row8 == j, rj.astype(jnp.int32), rank)
    eidx_ref[...] = eidx
    gate_ref[...] = gates
    rank_ref[...] = rank
    run = run_scr[...] + jnp.sum(oh_all, axis=1, keepdims=True)
    run_scr[...] = run
    cnt_ref[...] = jnp.broadcast_to(run, cnt_ref.shape)


def _route(logits_t):
    ne, n_tok = logits_t.shape
    t = ROUTE_TILE
    row = lambda dt: jax.ShapeDtypeStruct((8, n_tok), dt)
    spec = pl.BlockSpec((8, t), lambda i: (0, i))
    return pl.pallas_call(
        _route_kernel,
        grid=(n_tok // t,),
        in_specs=[pl.BlockSpec((ne, t), lambda i: (0, i))],
        out_specs=[spec, spec, spec, pl.BlockSpec((ne, LANES), lambda i: (0, 0))],
        out_shape=[row(jnp.int32), row(F32), row(jnp.int32), jax.ShapeDtypeStruct((ne, LANES), F32)],
        scratch_shapes=[pltpu.VMEM((ne, 1), F32)],
        compiler_params=_params(("arbitrary",)),
        name="moe_route",
    )(logits_t)


def _gather_rows(idx_smem, src_hbm, dst, sem, n_rows):
    def body(r, carry):
        pltpu.make_async_copy(src_hbm.at[idx_smem[r]], dst.at[r], sem).start()
        return carry
    lax.fori_loop(0, n_rows, body, 0)


def _wait_rows(src_hbm, dst, sem, n_rows):
    def body(r, carry):
        pltpu.make_async_copy(src_hbm.at[0], dst.at[r], sem).wait()
        return carry
    lax.fori_loop(0, n_rows, body, 0)


def _prefetched_gather(i, n_steps, idx_cur, idx_nxt, src_hbm, idx_smem, buf, idx_sem, row_sem, n_rows):
    def issue(idx_ref, slot):
        cp = pltpu.make_async_copy(idx_ref.at[0, 0], idx_smem, idx_sem)
        cp.start()
        cp.wait()
        _gather_rows(idx_smem, src_hbm, buf.at[slot], row_sem.at[slot], n_rows)

    @pl.when(i == 0)
    def _():
        issue(idx_cur, 0)

    @pl.when(i + 1 < n_steps)
    def _():
        issue(idx_nxt, (i + 1) % 2)

    _wait_rows(src_hbm, buf.at[i % 2], row_sem.at[i % 2], n_rows)


def _idx_specs(n_steps, n_rows, index_args):
    cur = pl.BlockSpec((1, 1, n_rows), lambda i, *_: (i, 0, 0))
    nxt = pl.BlockSpec((1, 1, n_rows), lambda i, *_: (jnp.minimum(i + 1, n_steps - 1), 0, 0))
    return [cur, nxt]


def _expert_kernel(n_blocks, be_ref, idx_cur, idx_nxt, h_hbm, wg_ref, bg_ref, wu_ref, bu_ref, wd_ref, bd_ref, o_ref,
                   idx_smem, xbuf, idx_sem, row_sem):
    i = pl.program_id(0)
    _prefetched_gather(i, n_blocks, idx_cur, idx_nxt, h_hbm, idx_smem, xbuf, idx_sem, row_sem, EXPERT_ROWS)
    x = xbuf[i % 2].astype(BF16)
    gt = jnp.minimum(_dg(x, wg_ref[...]) + bg_ref[...], SWIGLU_LIMIT)
    up = jnp.clip(_dg(x, wu_ref[...]) + bu_ref[...], -SWIGLU_LIMIT, SWIGLU_LIMIT)
    act = (up + 1.0) * gt * _sigmoid(SWIGLU_ALPHA * gt)
    o_ref[...] = _dg(act.astype(BF16), wd_ref[...]) + bd_ref[...]


def _experts(h2, row_tok, block_e, layer, wg, bg, wu, bu, wd, bd):
    n_blocks = row_tok.shape[0]
    d = h2.shape[1]
    ff = wg.shape[-1]
    wspec = lambda a, b: pl.BlockSpec((None, None, a, b), lambda i, be: (layer, be[i], 0, 0))
    return pl.pallas_call(
        functools.partial(_expert_kernel, n_blocks),
        grid_spec=pltpu.PrefetchScalarGridSpec(
            num_scalar_prefetch=1,
            grid=(n_blocks,),
            in_specs=_idx_specs(n_blocks, EXPERT_ROWS, 2) + [
                pl.BlockSpec(memory_space=pl.ANY),
                wspec(d, ff), wspec(1, ff), wspec(d, ff), wspec(1, ff), wspec(ff, d), wspec(1, d)],
            out_specs=pl.BlockSpec((EXPERT_ROWS, d), lambda i, be: (i, 0)),
            scratch_shapes=[pltpu.SMEM((EXPERT_ROWS,), jnp.int32), pltpu.VMEM((2, EXPERT_ROWS, d), F32),
                            pltpu.SemaphoreType.DMA(()), pltpu.SemaphoreType.DMA((2,))],
        ),
        out_shape=jax.ShapeDtypeStruct((n_blocks * EXPERT_ROWS, d), F32),
        compiler_params=_params(("arbitrary",)),
        name="moe_experts",
    )(block_e, row_tok.reshape(n_blocks, 1, EXPERT_ROWS), row_tok.reshape(n_blocks, 1, EXPERT_ROWS), h2,
      wg, bg, wu, bu, wd, bd)


COMBINE_TILE = 128


def _combine_kernel(n_steps, final, idx_cur, idx_nxt, ys_hbm, gate_ref, x1_ref, mod_ref, fg_ref, o_ref,
                    idx_smem, buf, idx_sem, row_sem):
    i = pl.program_id(0)
    tc = COMBINE_TILE
    _prefetched_gather(i, n_steps, idx_cur, idx_nxt, ys_hbm, idx_smem, buf, idx_sem, row_sem, TOP_K * tc)
    rows = buf.at[i % 2]
    g = gate_ref[...]
    moe = g[:, 0:1] * rows[0:tc, :]
    for j in range(1, TOP_K):
        moe = moe + g[:, j:j + 1] * rows[j * tc:(j + 1) * tc, :]
    x2 = x1_ref[...] + mod_ref[0, 5:6, :] * moe
    o_ref[...] = _rms(x2) * fg_ref[...] if final else x2


def _combine(dest_t, ys, gates_t, x1, mod, final_g, mod_row, final):
    n_tok, d = x1.shape
    tc = COMBINE_TILE
    n_steps = n_tok // tc
    per = TOK_TILE // tc
    return pl.pallas_call(
        functools.partial(_combine_kernel, n_steps, final),
        grid=(n_steps,),
        in_specs=_idx_specs(n_steps, TOP_K * tc, 1) + [
                  pl.BlockSpec(memory_space=pl.ANY),
                  pl.BlockSpec((tc, TOP_K), lambda i: (i, 0)),
                  pl.BlockSpec((tc, d), lambda i: (i, 0)),
                  pl.BlockSpec((1, 6, d), lambda i: (mod_row(i // per), 0, 0)),
                  pl.BlockSpec((1, d), lambda i: (0, 0))],
        out_specs=pl.BlockSpec((tc, d), lambda i: (i, 0)),
        out_shape=jax.ShapeDtypeStruct((n_tok, d), F32),
        scratch_shapes=[pltpu.SMEM((TOP_K * tc,), jnp.int32), pltpu.VMEM((2, TOP_K * tc, d), F32),
                        pltpu.SemaphoreType.DMA(()), pltpu.SemaphoreType.DMA((2,))],
        compiler_params=_params(("arbitrary",)),
        name="moe_combine",
    )(dest_t.reshape(n_steps, 1, TOP_K * tc), dest_t.reshape(n_steps, 1, TOP_K * tc), ys, gates_t, x1, mod,
      final_g.reshape(1, d))


def _moe(h2, logits_t, x1, mod, mod_row, layer, ew, final_g, final):
    n_tok = h2.shape[0]
    eidx, gates, rank, cnt = _route(logits_t)
    counts = cnt[:, 0].astype(jnp.int32)
    blk = EXPERT_ROWS
    padded = (counts + blk - 1) // blk * blk
    pad_end = jnp.cumsum(padded)
    pad_start = pad_end - padded
    e4 = eidx[0:TOP_K]
    dest = pad_start[e4] + rank[0:TOP_K]
    n_blocks = n_tok * TOP_K // blk + N_EXPERTS
    tok_ids = jnp.broadcast_to(jnp.arange(n_tok, dtype=jnp.int32)[None, :], dest.shape)
    row_tok = jnp.zeros((n_blocks * blk,), jnp.int32).at[dest.reshape(-1)].set(tok_ids.reshape(-1))
    block_e = jnp.minimum(jnp.searchsorted(pad_end, jnp.arange(n_blocks, dtype=jnp.int32) * blk, side="right"),
                          N_EXPERTS - 1).astype(jnp.int32)
    ys = _experts(h2, row_tok.reshape(n_blocks, blk), block_e, layer, *ew)
    tc = COMBINE_TILE
    dest_t = dest.reshape(TOP_K, n_tok // tc, tc).transpose(1, 0, 2).reshape(n_tok // tc, TOP_K * tc)
    return _combine(dest_t, ys, gates[0:TOP_K].T, x1, mod, final_g, mod_row, final)


def _layer_params(names, tensors, l):
    return {k: tensors[k][l] for k in names}


def kernel(x_prompt, x_sample, state_rwkv, state_lru, c, c_ctx, norm1_g, norm2_g, final_g, w_mod, b_mod, w_in, w_out,
           rw_mu, rw_w0, rw_w2, rw_a0, rw_a2, rw_g2, rw_kk, rw_ka, rw_rk, rw_lnx_g, rw_lnx_b,
           hy_conv_w, hy_conv_b, hy_f_w1, hy_f_b1, hy_f_w2, hy_f_b2, hy_f_w3, hy_f_b3,
           hy_freq, hy_decay, hy_bias, hy_out_g,
           lru_conv_w, lru_conv_b, lru_wa, lru_ba, lru_wx, lru_bx, lru_lam, lru_out_g,
           router_w, router_b, exp_w_gate, exp_b_gate, exp_w_up, exp_b_up, exp_w_down, exp_b_down):
    tensors = dict(locals())
    bp, sp, d = x_prompt.shape
    bl, sl, _ = x_sample.shape
    depth = w_in.shape[0]
    n_ctx, n_lat = bp * sp, bl * sl
    n_tok = n_ctx + n_lat
    assert sp % SEQ_BLOCK == 0 and sl % SEQ_BLOCK == 0 and n_ctx % sl == 0 and SEQ_BLOCK % GRID_W == 0
    assert n_tok % ROUTE_TILE == 0 and (n_tok * TOP_K) % EXPERT_ROWS == 0

    x = jnp.concatenate([x_prompt.reshape(n_ctx, d), x_sample.reshape(n_lat, d)], axis=0)
    n_rows = 16
    cc = jnp.concatenate([c_ctx[None, :], c, jnp.zeros((n_rows - 1 - bl, d), F32)], axis=0)
    mods = _modulation(cc, w_mod, b_mod).reshape(depth, n_rows, 6, d)
    ctx_tiles = n_ctx // TOK_TILE
    lat_tiles = sl // TOK_TILE
    mod_row = lambda i: jnp.where(i < ctx_tiles, 0, 1 + (i - ctx_tiles) // lat_tiles)

    ew = (exp_w_gate.astype(BF16), exp_b_gate[:, :, None, :], exp_w_up.astype(BF16), exp_b_up[:, :, None, :],
          exp_w_down.astype(BF16), exp_b_down[:, :, None, :])
    zero_rw = jnp.zeros((bp, 2, N_HEADS_A, HEAD, HEAD), F32)
    zero_lru = jnp.zeros((bp, 2, D_C), F32)
    nb_c, nb_l = sp // SEQ_BLOCK, sl // SEQ_BLOCK
    off_l = n_ctx // SEQ_BLOCK
    layer_names = [k for k, t in tensors.items() if k not in ("x_prompt", "x_sample", "state_rwkv", "state_lru", "c",
                                                              "c_ctx", "final_g", "w_mod", "b_mod", "exp_w_gate",
                                                              "exp_b_gate", "exp_w_up", "exp_b_up", "exp_w_down",
                                                              "exp_b_down")]
    new_rw, new_lru = [], []
    for l in range(depth):
        lp = _layer_params(layer_names, tensors, l)
        p = _in_proj(x, mods[l], lp["norm1_g"], lp["w_in"].astype(BF16), mod_row)
        rww, lrw, hyw = _rw_weights(lp), _lru_weights(lp), _hy_weights(lp)
        ya_c, s_rw = _rwkv_mixer(p, rww, zero_rw, 0, bp, nb_c, sp)
        ya_l, _ = _rwkv_mixer(p, rww, state_rwkv[:, l], off_l, bl, nb_l, GRID_W)
        yb_c = _hyena_mixer(p, hyw, 0, bp, sp, sp)
        yb_l = _hyena_mixer(p, hyw, n_ctx // sl, bl, sl, GRID_W)
        hf_c, hb_c, s_lru = _lru_mixer(p, lrw, zero_lru, 0, bp, nb_c, sp)
        hf_l, hb_l, _ = _lru_mixer(p, lrw, state_lru[:, l], off_l, bl, nb_l, GRID_W)
        cat = lambda a, b: jnp.concatenate([a, b], axis=0)
        x1, h2, logits_t = _out_proj(cat(ya_c, ya_l), cat(yb_c, yb_l), cat(hf_c, hf_l), cat(hb_c, hb_l), p, x,
                                     mods[l], lp, mod_row)
        x = _moe(h2, logits_t, x1, mods[l], mod_row, l, ew, final_g, l == depth - 1)
        new_rw.append(s_rw)
        new_lru.append(s_lru)
    y_prompt = x[0:n_ctx].reshape(bp, sp, d)
    y_sample = x[n_ctx:].reshape(bl, sl, d)
    return (y_prompt, y_sample, jnp.stack(new_rw, axis=1), jnp.stack(new_lru, axis=1))
```

```python
import functools
import math

import numpy as np
import jax
import jax.numpy as jnp
from jax import lax
from jax.experimental import pallas as pl
from jax.experimental.pallas import tpu as pltpu

F32 = jnp.float32
BF16 = jnp.bfloat16

GRID_W = 64
HEAD = 64
N_HEADS_A = 6
D_A = N_HEADS_A * HEAD
D_B = 256
D_C = 384
COLS_A = 3 * D_A + 384
COLS_B = 3 * D_B
COLS_C = 2 * D_C
N_EXPERTS = 32
TOP_K = 4
N_BANDS = 16
LRU_C = 8.0
SWIGLU_LIMIT = 7.0
SWIGLU_ALPHA = 1.702
EPS = 1e-6
LNX_EPS = 64e-5

LANES = 128
SEQ_BLOCK = 256
TOK_TILE = 256
CHUNK = 64
ROUTE_TILE = 512
EXPERT_ROWS = 512
GATHER_UNROLL = 16
DFT_UNROLL = 4
VMEM_LIMIT = 56 * 1024 * 1024

NN = (((1,), (0,)), ((), ()))
NT = (((1,), (1,)), ((), ()))
TN = (((0,), (0,)), ((), ()))


def _dg(a, b, dims=NN):
    return lax.dot_general(a, b, dims, preferred_element_type=F32)


def _dot1(a, b, dims=NN):
    return _dg(a.astype(BF16), b.astype(BF16), dims)


def _split(x):
    hi = x.astype(BF16)
    return hi, (x - hi.astype(F32)).astype(BF16)


def _dot3(a, b, dims=NN):
    ah, al = _split(a)
    bh, bl = _split(b)
    return _dg(ah, bh, dims) + (_dg(ah, bl, dims) + _dg(al, bh, dims))


def _dot_exact_rhs(a, b_bf16, dims=NN):
    a1 = a.astype(BF16)
    r1 = a - a1.astype(F32)
    a2 = r1.astype(BF16)
    a3 = (r1 - a2.astype(F32)).astype(BF16)
    return _dg(a1, b_bf16, dims) + (_dg(a2, b_bf16, dims) + _dg(a3, b_bf16, dims))


def _dot_exact_lhs(a_bf16, b, dims=NN):
    b1 = b.astype(BF16)
    r1 = b - b1.astype(F32)
    b2 = r1.astype(BF16)
    b3 = (r1 - b2.astype(F32)).astype(BF16)
    return _dg(a_bf16, b1, dims) + (_dg(a_bf16, b2, dims) + _dg(a_bf16, b3, dims))


def _iota(shape, axis):
    return lax.broadcasted_iota(jnp.int32, shape, axis)


def _head_ones(n):
    return ((_iota((n, n), 0) // HEAD) == (_iota((n, n), 1) // HEAD)).astype(BF16)


def _sigmoid(x):
    return 1.0 / (1.0 + jnp.exp(-x))


def _params(sem):
    return pltpu.CompilerParams(dimension_semantics=sem, vmem_limit_bytes=VMEM_LIMIT)


def _shifted(u, shift, pos, seg):
    n = u.shape[0]
    rolled = pltpu.roll(u, (-shift) % n, 0)
    ok = (pos + shift >= 0) & (pos + shift < seg)
    return jnp.where(ok, rolled, 0.0)


def _mod_kernel(c_ref, w_ref, b_ref, o_ref):
    c = c_ref[...]
    s = c * _sigmoid(c)
    o_ref[0] = _dot3(s, w_ref[0]) + b_ref[0]


def _modulation(cc, w_mod, b_mod):
    depth, d, n = w_mod.shape
    r = cc.shape[0]
    tn = 1024
    return pl.pallas_call(
        _mod_kernel,
        grid=(depth, n // tn),
        in_specs=[
            pl.BlockSpec((r, d), lambda l, j: (0, 0)),
            pl.BlockSpec((1, d, tn), lambda l, j: (l, 0, j)),
            pl.BlockSpec((1, 1, tn), lambda l, j: (l, 0, j)),
        ],
        out_specs=pl.BlockSpec((1, r, tn), lambda l, j: (l, 0, j)),
        out_shape=jax.ShapeDtypeStruct((depth, r, n), F32),
        compiler_params=_params(("arbitrary", "arbitrary")),
        name="modulation",
    )(cc, w_mod, b_mod.reshape(depth, 1, n))


def _rms(x):
    return x * lax.rsqrt(jnp.mean(x * x, axis=-1, keepdims=True) + EPS)


def _inproj_kernel(x_ref, mod_ref, g_ref, w_ref, o_ref):
    h = _rms(x_ref[...]) * g_ref[...]
    h = h * (1.0 + mod_ref[0, 1:2, :]) + mod_ref[0, 0:1, :]
    o_ref[...] = _dg(h.astype(BF16), w_ref[...])


def _in_proj(x, mod, g, w_bf16, mod_row):
    n_tok, d = x.shape
    n = w_bf16.shape[1]
    return pl.pallas_call(
        _inproj_kernel,
        grid=(n_tok // TOK_TILE,),
        in_specs=[
            pl.BlockSpec((TOK_TILE, d), lambda i: (i, 0)),
            pl.BlockSpec((1, 6, d), lambda i: (mod_row(i), 0, 0)),
            pl.BlockSpec((1, d), lambda i: (0, 0)),
            pl.BlockSpec((d, n), lambda i: (0, 0)),
        ],
        out_specs=pl.BlockSpec((TOK_TILE, n), lambda i: (i, 0)),
        out_shape=jax.ShapeDtypeStruct((n_tok, n), F32),
        compiler_params=_params(("arbitrary",)),
        name="in_proj",
    )(x, mod, g.reshape(1, d), w_bf16)


def _rw_pre_kernel(seg, p_ref, mu_ref, w0_ref, w2_ref, a0_ref, a2_ref, g2_ref, kkw_ref, ka_ref, rk_ref,
                   r_ref, v_ref, kk_ref, kd_ref, bb_ref, lw_ref, bonus_ref, gate_ref):
    p = p_ref[...]
    n = p.shape[0]
    pos = _iota((n, 1), 0) % seg
    ps = p + (0.5 * _shifted(p, -1, pos, seg) + 0.5 * _shifted(p, 1, pos, seg) - p) * mu_ref[...]
    r = ps[:, 0:D_A]
    k = ps[:, D_A:2 * D_A]
    v = ps[:, 2 * D_A:3 * D_A]
    lo = ps[:, 3 * D_A:3 * D_A + 256]
    g_lo = ps[:, 3 * D_A + 256:3 * D_A + 384]
    ones = _head_ones(D_A)
    kk = k * kkw_ref[...]
    kk = kk * lax.rsqrt(_dot_exact_rhs(kk * kk, ones) + 1e-12)
    tanh_wl = jnp.tanh(lo[:, 0:128])
    al = lo[:, 128:256]
    bonus = jnp.zeros_like(r)
    for d in range(2):
        x = w0_ref[d] + _dot3(tanh_wl, w2_ref[d])
        lw_ref[d] = -_sigmoid(x) * math.exp(-0.5)
        asig = _sigmoid(a0_ref[d] + _dot3(al, a2_ref[d]))
        kd = k * (1.0 + (asig - 1.0) * ka_ref[...])
        kd_ref[d] = kd
        bb_ref[d] = kk * asig
        bonus = bonus + _dot_exact_rhs(r * kd * rk_ref[...], ones)
    r_ref[...] = r
    v_ref[...] = v
    kk_ref[...] = kk
    bonus_ref[...] = bonus * v
    gate_ref[...] = _dot3(_sigmoid(g_lo), g2_ref[...])


def _rw_pre(p, lw, off, bsz, nb, seg):
    n = bsz * nb * SEQ_BLOCK
    tok = lambda b, i: (off + b * nb + i, 0)
    out_tok = lambda b, i: (b * nb + i, 0)
    out_tok2 = lambda b, i: (0, b * nb + i, 0)
    full = lambda *shape: pl.BlockSpec(shape, lambda b, i: (0,) * len(shape))
    one = jax.ShapeDtypeStruct((n, D_A), F32)
    two = jax.ShapeDtypeStruct((2, n, D_A), F32)
    s1 = pl.BlockSpec((SEQ_BLOCK, D_A), out_tok)
    s2 = pl.BlockSpec((2, SEQ_BLOCK, D_A), out_tok2)
    return pl.pallas_call(
        functools.partial(_rw_pre_kernel, seg),
        grid=(bsz, nb),
        in_specs=[
            pl.BlockSpec((SEQ_BLOCK, COLS_A), tok),
            full(1, COLS_A), full(2, 1, D_A), full(2, 128, D_A), full(2, 1, D_A), full(2, 128, D_A),
            full(128, D_A), full(1, D_A), full(1, D_A), full(1, D_A),
        ],
        out_specs=[s1, s1, s1, s2, s2, s2, s1, s1],
        out_shape=[one, one, one, two, two, two, one, one],
        compiler_params=_params(("arbitrary", "arbitrary")),
        name="rwkv_pre",
    )(p, lw["mu"], lw["w0"], lw["w2p"], lw["a0"], lw["a2p"], lw["g2"], lw["kk"], lw["ka"], lw["rk"])


_inv_dot = _dot1
_attn_dot = _dot1
_state_dot = _dot1
_dft_dot = _dot1


def _unit_tri_inverses(nmats, eye, r2, c2):
    b16 = (r2 // 16) == (c2 // 16)
    b32 = (r2 // 32) == (c2 // 32)
    pws = [jnp.where(b16, n, 0.0) for n in nmats]
    xs = [eye + p for p in pws]
    for _ in range(3):
        pws = [_inv_dot(p, p) for p in pws]
        xs = [x + _inv_dot(x, p) for x, p in zip(xs, pws)]
    for off_mask in (b32 & ~b16, ~b32):
        ts = [_inv_dot(x, jnp.where(off_mask, n, 0.0)) for x, n in zip(xs, nmats)]
        xs = [x + _inv_dot(t, x) for x, t in zip(xs, ts)]
    return xs


def _rw_scan_kernel(nb, rf_ref, vf_ref, kkf_ref, rb_ref, vb_ref, kkb_ref, kdf_ref, bbf_ref, lwf_ref,
                    kdb_ref, bbb_ref, lwb_ref, s0_ref, yf_ref, yb_ref, sfin_ref, s_scr):
    i = pl.program_id(1)

    @pl.when(i == 0)
    def _():
        s_scr[...] = s0_ref[0]

    c = CHUNK
    nsub = rf_ref.shape[0] // c
    n_pairs = N_HEADS_A // 2
    row = _iota((c, c), 0)
    col = _iota((c, c), 1)
    r2 = _iota((2 * c, 2 * c), 0)
    c2 = _iota((2 * c, 2 * c), 1)
    hmask = (r2 // c) == (c2 // c)
    tr = r2 % c
    tc = c2 % c
    eye = (r2 == c2).astype(F32)
    lane_lo = _iota((c, 2 * c), 1) < c
    tri = [(col <= row).astype(BF16), (col >= row).astype(BF16)]
    m_strict = [hmask & (tr > tc), hmask & (tr < tc)]
    m_incl = [hmask & (tr >= tc), hmask & (tr <= tc)]
    dirs = ((rf_ref, vf_ref, kkf_ref, kdf_ref, bbf_ref, lwf_ref, yf_ref),
            (rb_ref, vb_ref, kkb_ref, kdb_ref, bbb_ref, lwb_ref, yb_ref))

    def stack_masked(x):
        return jnp.concatenate([jnp.where(lane_lo, x, 0.0), jnp.where(lane_lo, 0.0, x)], axis=0)

    def twice(x):
        return jnp.concatenate([x, x], axis=0)

    def body(j, carry):
        chains = []
        for d, (r_ref, v_ref, kk_ref, kd_ref, bb_ref, lw_ref, y_ref) in enumerate(dirs):
            rev = d == 1
            sub = (nsub - 1 - j) if rev else j
            sl = pl.ds(pl.multiple_of(sub * c, c), c)
            lw = lw_ref[sl, :]
            cl = _dot_exact_lhs(tri[d], lw)
            tot = cl[0:1, :] if rev else cl[c - 1:c, :]
            e_out = jnp.exp(-cl)
            e_rest = jnp.exp(tot - cl)
            rt = r_ref[sl, :] * jnp.exp(cl)
            at = -kk_ref[sl, :] * jnp.exp(cl - lw)
            bb = bb_ref[sl, :]
            kd = kd_ref[sl, :]
            v = v_ref[sl, :]
            ptot = jnp.exp(tot)
            for pr in range(n_pairs):
                ls = slice(2 * c * pr, 2 * c * pr + 2 * c)
                chains.append(dict(
                    d=d, pr=pr, sl=sl, ls=ls, y_ref=y_ref, ptot=ptot[:, ls],
                    a_s=stack_masked(at[:, ls]), r_s=stack_masked(rt[:, ls]), v2=stack_masked(v[:, ls]),
                    bk=jnp.concatenate([twice((bb * e_out)[:, ls]), twice((kd * e_out)[:, ls])], axis=0),
                    bkp=jnp.concatenate([twice((bb * e_rest)[:, ls]), twice((kd * e_rest)[:, ls])], axis=0)))
        gs = [_attn_dot(jnp.concatenate([ch["a_s"], ch["r_s"]], axis=0), ch["bk"], NT) for ch in chains]
        tinvs = _unit_tri_inverses([jnp.where(m_strict[ch["d"]], g[0:2 * c, 0:2 * c], 0.0)
                                    for ch, g in zip(chains, gs)], eye, r2, c2)
        ss = [s_scr[ch["d"], ch["pr"]] for ch in chains]
        rhs = [_state_dot(ch["a_s"], s, NT)
               + _attn_dot(jnp.where(m_strict[ch["d"]], g[0:2 * c, 2 * c:4 * c], 0.0), ch["v2"])
               for ch, g, s in zip(chains, gs, ss)]
        ypart = [_state_dot(ch["r_s"], s, NT)
                 + _attn_dot(jnp.where(m_incl[ch["d"]], g[2 * c:4 * c, 2 * c:4 * c], 0.0), ch["v2"])
                 for ch, g, s in zip(chains, gs, ss)]
        us = [_attn_dot(t, x) for t, x in zip(tinvs, rhs)]
        ys = [yp + _attn_dot(jnp.where(m_incl[ch["d"]], g[2 * c:4 * c, 0:2 * c], 0.0), u)
              for ch, g, yp, u in zip(chains, gs, ypart, us)]
        upds = [_state_dot(jnp.concatenate([u, ch["v2"]], axis=0), ch["bkp"], TN) for ch, u in zip(chains, us)]
        for ch, y, s, upd in zip(chains, ys, ss, upds):
            ch["y_ref"][ch["sl"], ch["ls"]] = y[0:c, :] + y[c:2 * c, :]
            s_scr[ch["d"], ch["pr"]] = s * ch["ptot"] + jnp.where(hmask, upd, 0.0)
        return carry

    lax.fori_loop(0, nsub, body, 0)

    @pl.when(i == nb - 1)
    def _():
        sfin_ref[0] = s_scr[...]


def _rw_scan(pre, s0_pairs, bsz, nb):
    r, v, kk, kd, bb, lw = pre
    n = bsz * nb * SEQ_BLOCK
    fwd = pl.BlockSpec((SEQ_BLOCK, D_A), lambda b, i: (b * nb + i, 0))
    bwd = pl.BlockSpec((SEQ_BLOCK, D_A), lambda b, i: (b * nb + nb - 1 - i, 0))
    fwd2 = pl.BlockSpec((None, SEQ_BLOCK, D_A), lambda b, i: (0, b * nb + i, 0))
    bwd2 = pl.BlockSpec((None, SEQ_BLOCK, D_A), lambda b, i: (1, b * nb + nb - 1 - i, 0))
    sst = pl.BlockSpec((1, 2, 3, 2 * HEAD, 2 * HEAD), lambda b, i: (b, 0, 0, 0, 0))
    y_shape = jax.ShapeDtypeStruct((n, D_A), F32)
    return pl.pallas_call(
        functools.partial(_rw_scan_kernel, nb),
        grid=(bsz, nb),
        in_specs=[fwd, fwd, fwd, bwd, bwd, bwd, fwd2, fwd2, fwd2, bwd2, bwd2, bwd2, sst],
        out_specs=[fwd, bwd, sst],
        out_shape=[y_shape, y_shape, jax.ShapeDtypeStruct((bsz, 2, 3, 2 * HEAD, 2 * HEAD), F32)],
        scratch_shapes=[pltpu.VMEM((2, 3, 2 * HEAD, 2 * HEAD), F32)],
        compiler_params=_params(("arbitrary", "arbitrary")),
        name="rwkv_scan",
    )(r, v, kk, r, v, kk, kd, bb, lw, kd, bb, lw, s0_pairs)


def _to_pairs(s):
    lead = s.shape[:-3]
    s = s.reshape(lead + (3, 2, HEAD, HEAD))
    z = jnp.zeros_like(s[..., 0, :, :])
    top = jnp.concatenate([s[..., 0, :, :], z], axis=-1)
    bot = jnp.concatenate([z, s[..., 1, :, :]], axis=-1)
    return jnp.concatenate([top, bot], axis=-2)


def _from_pairs(sp):
    lead = sp.shape[:-3]
    return jnp.stack([sp[..., :HEAD, :HEAD], sp[..., HEAD:, HEAD:]], axis=-3).reshape(lead + (6, HEAD, HEAD))


def _lru_kernel(rev, nb, seg, p_ref, cw_ref, cb_ref, wg_ref, bg_ref, lam_ref, h0_ref, h_ref, hfin_ref, hc_scr):
    i = pl.program_id(1)

    @pl.when(i == 0)
    def _():
        hc_scr[...] = h0_ref[0]

    u_in = p_ref[...]
    n = u_in.shape[0]
    t = _iota((n, 1), 0)
    pos = t % seg
    cw = cw_ref[...]
    xm = (cw[0:1] * _shifted(u_in, -2, pos, seg) + cw[1:2] * _shifted(u_in, -1, pos, seg)
          + cw[2:3] * u_in + cw[3:4] * _shifted(u_in, 1, pos, seg)) + cb_ref[...]
    g = _dot3(xm, wg_ref[...]) + bg_ref[...]
    gr = _sigmoid(g[:, 0:D_C])
    gi = _sigmoid(g[:, D_C:2 * D_C])
    lam = lam_ref[...]
    softplus_neg_lam = jnp.maximum(-lam, 0.0) + jnp.log1p(jnp.exp(-jnp.abs(lam)))
    log_a = -LRU_C * gr * softplus_neg_lam
    a = jnp.exp(log_a)
    u = jnp.sqrt(-jnp.tanh(log_a) * (a * a + 1.0)) * gi * xm
    s = 1
    while s < n:
        if rev:
            ok = t < n - s
            a_sh = jnp.where(ok, pltpu.roll(a, n - s, 0), 1.0)
            u_sh = jnp.where(ok, pltpu.roll(u, n - s, 0), 0.0)
        else:
            ok = t >= s
            a_sh = jnp.where(ok, pltpu.roll(a, s, 0), 1.0)
            u_sh = jnp.where(ok, pltpu.roll(u, s, 0), 0.0)
        u = a * u_sh + u
        a = a * a_sh
        s *= 2
    h = a * hc_scr[...] + u
    h_ref[...] = h
    hc_scr[...] = h[0:1, :] if rev else h[n - 1:n, :]

    @pl.when(i == nb - 1)
    def _():
        hfin_ref[0] = hc_scr[...]


def _lru(p, lw, h0, d, off, bsz, nb, seg):
    rev = d == 1
    n = bsz * nb * SEQ_BLOCK
    blk = (lambda i: nb - 1 - i) if rev else (lambda i: i)
    pcol = (COLS_A + COLS_B) // D_C
    full = lambda *shape: pl.BlockSpec(shape, lambda b, i: (0,) * len(shape))
    dsel = lambda *shape: pl.BlockSpec((None,) + shape, lambda b, i: (d,) + (0,) * len(shape))
    return pl.pallas_call(
        functools.partial(_lru_kernel, rev, nb, seg),
        grid=(bsz, nb),
        in_specs=[
            pl.BlockSpec((SEQ_BLOCK, D_C), lambda b, i: (off + b * nb + blk(i), pcol)),
            full(4, D_C), full(1, D_C), dsel(D_C, 2 * D_C), dsel(1, 2 * D_C), dsel(1, D_C),
            pl.BlockSpec((1, None, 1, D_C), lambda b, i: (b, d, 0, 0)),
        ],
        out_specs=[pl.BlockSpec((SEQ_BLOCK, D_C), lambda b, i: (b * nb + blk(i), 0)),
                   pl.BlockSpec((1, 1, D_C), lambda b, i: (b, 0, 0))],
        out_shape=[jax.ShapeDtypeStruct((n, D_C), F32), jax.ShapeDtypeStruct((bsz, 1, D_C), F32)],
        scratch_shapes=[pltpu.VMEM((1, D_C), F32)],
        compiler_params=_params(("arbitrary", "arbitrary")),
        name="rglru_bwd" if rev else "rglru_fwd",
    )(p, lw["cw"], lw["cb"], lw["wg"], lw["bg"], lw["lam"], h0)


def _rw_post_kernel(yf_ref, yb_ref, bonus_ref, gate_ref, g_ref, b_ref, o_ref):
    y = yf_ref[...] + yb_ref[...]
    ones = _head_ones(D_A)
    mean = _dot_exact_rhs(y, ones) * (1.0 / HEAD)
    yc = y - mean
    var = _dot_exact_rhs(yc * yc, ones) * (1.0 / HEAD)
    yn = yc * lax.rsqrt(var + LNX_EPS) * g_ref[...] + b_ref[...]
    o_ref[...] = (yn + bonus_ref[...]) * gate_ref[...]


def _rw_post(yf, yb, bonus, gate, lw):
    n = yf.shape[0]
    s1 = pl.BlockSpec((TOK_TILE, D_A), lambda i: (i, 0))
    w1 = pl.BlockSpec((1, D_A), lambda i: (0, 0))
    return pl.pallas_call(
        _rw_post_kernel,
        grid=(n // TOK_TILE,),
        in_specs=[s1, s1, s1, s1, w1, w1],
        out_specs=s1,
        out_shape=jax.ShapeDtypeStruct((n, D_A), F32),
        compiler_params=_params(("arbitrary",)),
        name="rwkv_post",
    )(yf, yb, bonus, gate, lw["lnx_g"], lw["lnx_b"])


def _rw_weights(lp):
    def pad_dir(w, slot):
        z = jnp.zeros((2, 128, D_A), F32)
        z = z.at[0, 0:64].set(w[0])
        return z.at[1, 64:128].set(w[1])
    return {
        "mu": lp["rw_mu"].reshape(1, COLS_A),
        "w0": lp["rw_w0"].reshape(2, 1, D_A), "w2p": pad_dir(lp["rw_w2"], 0),
        "a0": lp["rw_a0"].reshape(2, 1, D_A), "a2p": pad_dir(lp["rw_a2"], 1),
        "g2": lp["rw_g2"], "kk": lp["rw_kk"].reshape(1, D_A), "ka": lp["rw_ka"].reshape(1, D_A),
        "rk": lp["rw_rk"].reshape(1, D_A),
        "lnx_g": lp["rw_lnx_g"].reshape(1, D_A), "lnx_b": lp["rw_lnx_b"].reshape(1, D_A),
    }


def _rwkv_mixer(p, lw, s0, off, bsz, nb, seg):
    r, v, kk, kd, bb, lwd, bonus, gate = _rw_pre(p, lw, off, bsz, nb, seg)
    yf, yb, sfin = _rw_scan((r, v, kk, kd, bb, lwd), _to_pairs(s0), bsz, nb)
    return _rw_post(yf, yb, bonus, gate, lw), _from_pairs(sfin)


def _lru_weights(lp):
    def block_diag(w):
        eye = jnp.eye(N_HEADS_A, dtype=F32)
        return jnp.einsum("hij,hg->higj", w, eye).reshape(D_C, D_C)
    wg = jnp.stack([jnp.concatenate([block_diag(lp["lru_wa"][d]), block_diag(lp["lru_wx"][d])], axis=1)
                    for d in range(2)])
    bg = jnp.stack([jnp.concatenate([lp["lru_ba"][d], lp["lru_bx"][d]])[None] for d in range(2)])
    return {"cw": lp["lru_conv_w"], "cb": lp["lru_conv_b"].reshape(1, D_C), "wg": wg, "bg": bg,
            "lam": lp["lru_lam"].reshape(2, 1, D_C)}


def _lru_mixer(p, lw, h0, off, bsz, nb, seg):
    h0 = h0.reshape(bsz, 2, 1, D_C)
    hf, ff = _lru(p, lw, h0, 0, off, bsz, nb, seg)
    hb, fb = _lru(p, lw, h0, 1, off, bsz, nb, seg)
    return hf, hb, jnp.concatenate([ff, fb], axis=1)


def _dft_tables(seq):
    n = 2 * seq
    n2 = 128 if n >= 4096 else 16
    n1 = n // n2
    n1h = n1 // 2
    i1 = np.arange(n1h)[None, None, :]
    k1 = np.arange(n1)[None, :, None]
    i2 = np.arange(n2)[:, None, None]
    ph = 2.0 * np.pi * (i1 * k1 / n1 + i2 * k1 / n)
    f1 = np.concatenate([np.cos(ph), -np.sin(ph)], axis=1)
    g3 = np.concatenate([np.cos(ph), -np.sin(ph)], axis=1).transpose(0, 2, 1) / n
    a = 2.0 * np.pi * np.outer(np.arange(n2), np.arange(n2)) / n2
    f2r, f2i = np.cos(a), -np.sin(a)
    f2 = np.block([[f2r, -f2i], [f2i, f2r]])
    f2c = np.block([[f2r, f2i], [-f2i, f2r]])
    as32 = lambda t: jnp.asarray(t, F32)
    return n1, n2, as32(f1), as32(f2), as32(f2c), as32(g3)


def _dft_stage1(src_ref, dst_ref, f1_ref, n1, n2):
    n1h = n1 // 2

    def body(i, carry):
        xs = src_ref[pl.ds(i, n1h, stride=n2), :]
        a = _dft_dot(f1_ref[i], xs)
        dst_ref[pl.ds(i, n1, stride=2 * n2), :] = a[0:n1]
        dst_ref[pl.ds(n2 + i, n1, stride=2 * n2), :] = a[n1:2 * n1]
        return carry

    lax.fori_loop(0, n2, body, 0, unroll=DFT_UNROLL)


def _hy_filter_kernel(n1, n2, z_ref, w1_ref, b1_ref, w2_ref, b2_ref, w3f_ref, w3b_ref, b3f_ref, b3b_ref,
                      dcf_ref, dcb_ref, freq_ref, f1_ref, f2_ref, o_ref, hf_scr, hb_scr, xf_scr, xb_scr):
    z = z_ref[...]
    h = jnp.sin(freq_ref[0:1, :] * (_dot3(z, w1_ref[...]) + b1_ref[...]))
    h = jnp.sin(freq_ref[1:2, :] * (_dot3(h, w2_ref[...]) + b2_ref[...]))
    t = z[:, 0:1]
    hf_scr[...] = (_dot3(h, w3f_ref[...]) + b3f_ref[...]) * jnp.exp(-t * jnp.abs(dcf_ref[...]))
    hb = (_dot3(h, w3b_ref[...]) + b3b_ref[...]) * jnp.exp(-t * jnp.abs(dcb_ref[...]))
    hb_scr[...] = jnp.where(_iota(hb.shape, 0) == 0, 0.0, hb)
    _dft_stage1(hf_scr, xf_scr, f1_ref, n1, n2)
    _dft_stage1(hb_scr, xb_scr, f1_ref, n1, n2)
    f2 = f2_ref[...]

    def body(k, carry):
        sl = pl.ds(pl.multiple_of(k * 2 * n2, 2 * n2), 2 * n2)
        xf = _dot3(f2, xf_scr[sl, :])
        xb = _dot3(f2, xb_scr[sl, :])
        o_ref[sl, :] = jnp.concatenate([xf[0:n2] + xb[0:n2], xf[n2:2 * n2] - xb[n2:2 * n2]], axis=0)
        return carry

    lax.fori_loop(0, n1, body, 0)


def _hy_filter(lw, seq, tables):
    n1, n2, f1, f2, _, _ = tables
    ct = LANES
    nct = D_B // ct
    full = lambda a: pl.BlockSpec(a.shape, lambda j: (0,) * a.ndim)
    col = lambda rows, shift: pl.BlockSpec((rows, ct), lambda j: (0, shift + j))
    pos = jnp.arange(seq, dtype=F32)
    t = pos[:, None] / (seq - 1)
    bands = jnp.linspace(1e-4, N_BANDS - 1, N_BANDS, dtype=F32)
    ang = (2.0 * math.pi / seq) * pos[:, None] * bands[None, :]
    z = jnp.concatenate([t, jnp.cos(ang), -jnp.sin(ang), jnp.zeros((seq, 64 - 1 - 2 * N_BANDS), F32)], axis=-1)
    args = [z, lw["w1p"], lw["b1"], lw["w2"], lw["b2"]]
    return pl.pallas_call(
        functools.partial(_hy_filter_kernel, n1, n2),
        grid=(nct,),
        in_specs=[full(a) for a in args] + [
            col(64, 0), col(64, nct), col(1, 0), col(1, nct), col(1, 0), col(1, nct),
            full(lw["freq"]), full(f1), full(f2)],
        out_specs=pl.BlockSpec((4 * seq, ct), lambda j: (0, j)),
        out_shape=jax.ShapeDtypeStruct((4 * seq, D_B), F32),
        scratch_shapes=[pltpu.VMEM((seq, ct), F32), pltpu.VMEM((seq, ct), F32),
                        pltpu.VMEM((4 * seq, ct), F32), pltpu.VMEM((4 * seq, ct), F32)],
        compiler_params=_params(("arbitrary",)),
        name="hyena_filter",
    )(*args, lw["w3"], lw["w3"], lw["b3"], lw["b3"], lw["decay"], lw["decay"], lw["freq"], f1, f2)


def _hyena_kernel(n1, n2, seg, pv_ref, px1_ref, px2_ref, cwv_ref, cwx1_ref, cwx2_ref, cbv_ref, cbx1_ref, cbx2_ref,
                  bias_ref, h_ref, f1_ref, f2_ref, f2c_ref, g3_ref, o_ref, z_scr, x_scr, y_scr):
    seq = pv_ref.shape[0]
    pos = _iota((seq, 1), 0) % seg

    def conv(p_ref, w_ref, b_ref):
        p = p_ref[...]
        w = w_ref[...]
        return w[0:1] * _shifted(p, -1, pos, seg) + w[1:2] * p + w[2:3] * _shifted(p, 1, pos, seg) + b_ref[...]

    z_scr[...] = conv(pv_ref, cwv_ref, cbv_ref) * conv(px1_ref, cwx1_ref, cbx1_ref)
    _dft_stage1(z_scr, x_scr, f1_ref, n1, n2)
    f2 = f2_ref[...]
    f2c = f2c_ref[...]

    def mid(k, carry):
        sl = pl.ds(pl.multiple_of(k * 2 * n2, 2 * n2), 2 * n2)
        x = _dft_dot(f2, x_scr[sl, :])
        hh = h_ref[sl, :]
        xr, xi = x[0:n2], x[n2:2 * n2]
        hr, hi = hh[0:n2], hh[n2:2 * n2]
        y = jnp.concatenate([xr * hr - xi * hi, xr * hi + xi * hr], axis=0)
        x_scr[sl, :] = _dft_dot(f2c, y)
        return carry

    lax.fori_loop(0, n1, mid, 0, unroll=DFT_UNROLL)
    n1h = n1 // 2

    def last(i, carry):
        d = jnp.concatenate([x_scr[pl.ds(i, n1, stride=2 * n2), :], x_scr[pl.ds(n2 + i, n1, stride=2 * n2), :]], axis=0)
        y_scr[pl.ds(i, n1h, stride=n2), :] = _dft_dot(g3_ref[i], d)
        return carry

    lax.fori_loop(0, n2, last, 0, unroll=DFT_UNROLL)
    o_ref[...] = conv(px2_ref, cwx2_ref, cbx2_ref) * (y_scr[...] + z_scr[...] * bias_ref[...])


def _hyena(p, lw, hspec, tables, off_seq, bsz, seq, seg):
    n1, n2, f1, f2, f2c, g3 = tables
    ct = LANES
    nct = D_B // ct
    c0 = COLS_A // ct
    pcol = lambda g: pl.BlockSpec((seq, ct), lambda j, b: (off_seq + b, c0 + g * nct + j))
    wcol = lambda rows, g: pl.BlockSpec((rows, ct), lambda j, b: (0, g * nct + j))
    once = lambda a: pl.BlockSpec(a.shape, lambda j, b: (0,) * a.ndim)
    return pl.pallas_call(
        functools.partial(_hyena_kernel, n1, n2, seg),
        grid=(nct, bsz),
        in_specs=[pcol(0), pcol(1), pcol(2), wcol(3, 0), wcol(3, 1), wcol(3, 2), wcol(1, 0), wcol(1, 1), wcol(1, 2),
                  wcol(1, 0), pl.BlockSpec((4 * seq, ct), lambda j, b: (0, j)),
                  once(f1), once(f2), once(f2c), once(g3)],
        out_specs=pl.BlockSpec((seq, ct), lambda j, b: (b, j)),
        out_shape=jax.ShapeDtypeStruct((bsz * seq, D_B), F32),
        scratch_shapes=[pltpu.VMEM((seq, ct), F32), pltpu.VMEM((4 * seq, ct), F32), pltpu.VMEM((seq, ct), F32)],
        compiler_params=_params(("arbitrary", "arbitrary")),
        name="hyena",
    )(p, p, p, lw["cw"], lw["cw"], lw["cw"], lw["cb"], lw["cb"], lw["cb"], lw["bias"], hspec, f1, f2, f2c, g3)


def _hy_weights(lp):
    pos_dim = 1 + 2 * N_BANDS
    return {
        "w1p": jnp.zeros((64, 64), F32).at[0:pos_dim].set(lp["hy_f_w1"]), "b1": lp["hy_f_b1"].reshape(1, 64),
        "w2": lp["hy_f_w2"], "b2": lp["hy_f_b2"].reshape(1, 64),
        "w3": lp["hy_f_w3"], "b3": lp["hy_f_b3"].reshape(1, 2 * D_B), "decay": lp["hy_decay"].reshape(1, 2 * D_B),
        "freq": lp["hy_freq"], "cw": lp["hy_conv_w"], "cb": lp["hy_conv_b"].reshape(1, COLS_B),
        "bias": lp["hy_bias"].reshape(1, D_B),
    }


def _hyena_mixer(p, lw, off_seq, bsz, seq, seg):
    tables = _dft_tables(seq)
    hspec = _hy_filter(lw, seq, tables)
    return _hyena(p, lw, hspec, tables, off_seq, bsz, seq, seg)


def _gelu_tanh(x):
    return 0.5 * x * (1.0 + jnp.tanh(math.sqrt(2.0 / math.pi) * (x + 0.044715 * (x * x * x))))


def _outproj_kernel(ya_ref, yb_ref, hf_ref, hb_ref, yg_ref, x_ref, mod_ref, w_ref, gb_ref, gc_ref, g2_ref,
                    rw_ref, rb_ref, x1_ref, h2_ref, lg_ref):
    ybn = _rms(yb_ref[...]) * gb_ref[...]
    ycn = _rms((hf_ref[...] + hb_ref[...]) * _gelu_tanh(yg_ref[...])) * gc_ref[...]
    y = (_dot1(ya_ref[...], w_ref[0:D_A, :]) + _dot1(ybn, w_ref[D_A:D_A + D_B, :])
         + _dot1(ycn, w_ref[D_A + D_B:D_A + D_B + D_C, :]))
    x1 = x_ref[...] + mod_ref[0, 2:3, :] * y
    x1_ref[...] = x1
    h2 = _rms(x1) * g2_ref[...] * (1.0 + mod_ref[0, 4:5, :]) + mod_ref[0, 3:4, :]
    _to_row_tiles(h2_ref, h2)
    lg_ref[...] = _dot3(rw_ref[...], h2, NT) + rb_ref[...]


def _out_proj(ya, yb, hf, hb, p, x, mod, lp, mod_row):
    n_tok, d = x.shape
    tile = lambda w: pl.BlockSpec((TOK_TILE, w), lambda i: (i, 0))
    full = lambda *shape: pl.BlockSpec(shape, lambda i: (0,) * len(shape))
    yg_col = (COLS_A + COLS_B + D_C) // D_C
    return pl.pallas_call(
        _outproj_kernel,
        grid=(n_tok // TOK_TILE,),
        in_specs=[tile(D_A), tile(D_B), tile(D_C), tile(D_C),
                  pl.BlockSpec((TOK_TILE, D_C), lambda i: (i, yg_col)), tile(d),
                  pl.BlockSpec((1, 6, d), lambda i: (mod_row(i), 0, 0)),
                  full(d, d), full(1, D_B), full(1, D_C), full(1, d), full(N_EXPERTS, d), full(N_EXPERTS, 1)],
        out_specs=[tile(d), pl.BlockSpec((TOK_TILE * ROW_TILE, LANES), lambda i: (i, 0)),
                   pl.BlockSpec((N_EXPERTS, TOK_TILE), lambda i: (0, i))],
        out_shape=[jax.ShapeDtypeStruct((n_tok, d), F32), jax.ShapeDtypeStruct((n_tok * ROW_TILE, LANES), F32),
                   jax.ShapeDtypeStruct((N_EXPERTS, n_tok), F32)],
        compiler_params=_params(("arbitrary",)),
        name="out_proj",
    )(ya, yb, hf, hb, p, x, mod, lp["w_out"].astype(BF16), lp["hy_out_g"].reshape(1, D_B),
      lp["lru_out_g"].reshape(1, D_C), lp["norm2_g"].reshape(1, d), lp["router_w"].T,
      lp["router_b"].reshape(N_EXPERTS, 1))


def _route_kernel(lg_ref, eidx_ref, gate_ref, rank_ref, cnt_ref, run_scr):
    i = pl.program_id(0)

    @pl.when(i == 0)
    def _():
        run_scr[...] = jnp.zeros_like(run_scr)

    l = lg_ref[...]
    ne, t = l.shape
    rowi = _iota((ne, t), 0)
    row8 = _iota((8, t), 0)
    tops, hots = [], []
    oh_all = jnp.zeros((ne, t), F32)
    eidx = jnp.zeros((8, t), jnp.int32)
    for j in range(TOP_K):
        m = jnp.max(l, axis=0, keepdims=True)
        idx = jnp.min(jnp.where(l == m, rowi, ne), axis=0, keepdims=True)
        oh = rowi == idx
        tops.append(m)
        hots.append(oh)
        oh_all = oh_all + oh.astype(F32)
        eidx = jnp.where(row8 == j, idx, eidx)
        l = jnp.where(oh, -jnp.inf, l)
    es = [jnp.exp(m - tops[0]) for m in tops]
    denom = es[0] + es[1] + es[2] + es[3]
    before = (_iota((t, t), 0) < _iota((t, t), 1)).astype(BF16)
    cum = _dg(oh_all.astype(BF16), before) + run_scr[...]
    gates = jnp.zeros((8, t), F32)
    rank = jnp.zeros((8, t), jnp.int32)
    for j in range(TOP_K):
        gates = jnp.where(row8 == j, es[j] / denom, gates)
        rj = jnp.sum(jnp.where(hots[j], cum, 0.0), axis=0, keepdims=True)
        rank = jnp.where(row8 == j, rj.astype(jnp.int32), rank)
    eidx_ref[...] = eidx
    gate_ref[...] = gates
    rank_ref[...] = rank
    run = run_scr[...] + jnp.sum(oh_all, axis=1, keepdims=True)
    run_scr[...] = run
    cnt_ref[...] = jnp.broadcast_to(run, cnt_ref.shape)


def _route(logits_t):
    ne, n_tok = logits_t.shape
    t = ROUTE_TILE
    row = lambda dt: jax.ShapeDtypeStruct((8, n_tok), dt)
    spec = pl.BlockSpec((8, t), lambda i: (0, i))
    return pl.pallas_call(
        _route_kernel,
        grid=(n_tok // t,),
        in_specs=[pl.BlockSpec((ne, t), lambda i: (0, i))],
        out_specs=[spec, spec, spec, pl.BlockSpec((ne, LANES), lambda i: (0, 0))],
        out_shape=[row(jnp.int32), row(F32), row(jnp.int32), jax.ShapeDtypeStruct((ne, LANES), F32)],
        scratch_shapes=[pltpu.VMEM((ne, 1), F32)],
        compiler_params=_params(("arbitrary",)),
        name="moe_route",
    )(logits_t)


ROW_TILE = 8


def _to_row_tiles(ref, x):
    n = x.shape[0]
    for s in range(ROW_TILE):
        ref[pl.ds(s, n, stride=ROW_TILE), :] = x[:, s * LANES:(s + 1) * LANES]


def _from_row_tiles(ref, start, n):
    return jnp.concatenate([ref[pl.ds(start * ROW_TILE + s, n, stride=ROW_TILE), :] for s in range(ROW_TILE)],
                           axis=1)


def _row_tile(ref, r):
    return ref.at[pl.ds(pl.multiple_of(r * ROW_TILE, ROW_TILE), ROW_TILE), :]


def _gather_rows(idx_smem, src_hbm, dst, sem, n_rows):
    def body(r, carry):
        pltpu.make_async_copy(_row_tile(src_hbm, idx_smem[r]), _row_tile(dst, r), sem).start()
        return carry
    lax.fori_loop(0, n_rows, body, 0, unroll=GATHER_UNROLL)


def _wait_rows(src_hbm, dst, sem, n_rows):
    def body(r, carry):
        pltpu.make_async_copy(_row_tile(src_hbm, 0), _row_tile(dst, r), sem).wait()
        return carry
    lax.fori_loop(0, n_rows, body, 0, unroll=GATHER_UNROLL)


def _prefetched_gather(i, n_steps, idx_cur, idx_nxt, src_hbm, idx_smem, buf, idx_sem, row_sem, n_rows):
    def issue(idx_ref, slot):
        cp = pltpu.make_async_copy(idx_ref.at[0, 0], idx_smem, idx_sem)
        cp.start()
        cp.wait()
        _gather_rows(idx_smem, src_hbm, buf.at[slot], row_sem.at[slot], n_rows)

    @pl.when(i == 0)
    def _():
        issue(idx_cur, 0)

    @pl.when(i + 1 < n_steps)
    def _():
        issue(idx_nxt, (i + 1) % 2)

    _wait_rows(src_hbm, buf.at[i % 2], row_sem.at[i % 2], n_rows)


def _idx_specs(n_steps, n_rows, index_args):
    cur = pl.BlockSpec((1, 1, n_rows), lambda i, *_: (i, 0, 0))
    nxt = pl.BlockSpec((1, 1, n_rows), lambda i, *_: (jnp.minimum(i + 1, n_steps - 1), 0, 0))
    return [cur, nxt]


def _expert_kernel(n_blocks, be_ref, idx_cur, idx_nxt, h_hbm, wg_ref, bg_ref, wu_ref, bu_ref, wd_ref, bd_ref, o_ref,
                   idx_smem, xbuf, idx_sem, row_sem):
    i = pl.program_id(0)
    _prefetched_gather(i, n_blocks, idx_cur, idx_nxt, h_hbm, idx_smem, xbuf, idx_sem, row_sem, EXPERT_ROWS)
    x = _from_row_tiles(xbuf.at[i % 2], 0, EXPERT_ROWS).astype(BF16)
    gt = jnp.minimum(_dg(x, wg_ref[...]) + bg_ref[...], SWIGLU_LIMIT)
    up = jnp.clip(_dg(x, wu_ref[...]) + bu_ref[...], -SWIGLU_LIMIT, SWIGLU_LIMIT)
    act = (up + 1.0) * gt * _sigmoid(SWIGLU_ALPHA * gt)
    _to_row_tiles(o_ref, _dg(act.astype(BF16), wd_ref[...]) + bd_ref[...])


def _experts(h2, row_tok, block_e, layer, wg, bg, wu, bu, wd, bd):
    n_blocks = row_tok.shape[0]
    d = ROW_TILE * LANES
    ff = wg.shape[-1]
    wspec = lambda a, b: pl.BlockSpec((None, None, a, b), lambda i, be: (layer, be[i], 0, 0))
    return pl.pallas_call(
        functools.partial(_expert_kernel, n_blocks),
        grid_spec=pltpu.PrefetchScalarGridSpec(
            num_scalar_prefetch=1,
            grid=(n_blocks,),
            in_specs=_idx_specs(n_blocks, EXPERT_ROWS, 2) + [
                pl.BlockSpec(memory_space=pl.ANY),
                wspec(d, ff), wspec(1, ff), wspec(d, ff), wspec(1, ff), wspec(ff, d), wspec(1, d)],
            out_specs=pl.BlockSpec((EXPERT_ROWS * ROW_TILE, LANES), lambda i, be: (i, 0)),
            scratch_shapes=[pltpu.SMEM((EXPERT_ROWS,), jnp.int32),
                            pltpu.VMEM((2, EXPERT_ROWS * ROW_TILE, LANES), F32),
                            pltpu.SemaphoreType.DMA(()), pltpu.SemaphoreType.DMA((2,))],
        ),
        out_shape=jax.ShapeDtypeStruct((n_blocks * EXPERT_ROWS * ROW_TILE, LANES), F32),
        compiler_params=_params(("arbitrary",)),
        name="moe_experts",
    )(block_e, row_tok.reshape(n_blocks, 1, EXPERT_ROWS), row_tok.reshape(n_blocks, 1, EXPERT_ROWS), h2,
      wg, bg, wu, bu, wd, bd)


COMBINE_TILE = 128


def _combine_kernel(n_steps, final, idx_cur, idx_nxt, ys_hbm, gate_ref, x1_ref, mod_ref, fg_ref, o_ref,
                    idx_smem, buf, idx_sem, row_sem):
    i = pl.program_id(0)
    tc = COMBINE_TILE
    _prefetched_gather(i, n_steps, idx_cur, idx_nxt, ys_hbm, idx_smem, buf, idx_sem, row_sem, TOP_K * tc)
    rows = buf.at[i % 2]
    g = gate_ref[...]
    moe = g[:, 0:1] * _from_row_tiles(rows, 0, tc)
    for j in range(1, TOP_K):
        moe = moe + g[:, j:j + 1] * _from_row_tiles(rows, j * tc, tc)
    x2 = x1_ref[...] + mod_ref[0, 5:6, :] * moe
    o_ref[...] = _rms(x2) * fg_ref[...] if final else x2


def _combine(dest_t, ys, gates_t, x1, mod, final_g, mod_row, final):
    n_tok, d = x1.shape
    tc = COMBINE_TILE
    n_steps = n_tok // tc
    per = TOK_TILE // tc
    return pl.pallas_call(
        functools.partial(_combine_kernel, n_steps, final),
        grid=(n_steps,),
        in_specs=_idx_specs(n_steps, TOP_K * tc, 1) + [
                  pl.BlockSpec(memory_space=pl.ANY),
                  pl.BlockSpec((tc, TOP_K), lambda i: (i, 0)),
                  pl.BlockSpec((tc, d), lambda i: (i, 0)),
                  pl.BlockSpec((1, 6, d), lambda i: (mod_row(i // per), 0, 0)),
                  pl.BlockSpec((1, d), lambda i: (0, 0))],
        out_specs=pl.BlockSpec((tc, d), lambda i: (i, 0)),
        out_shape=jax.ShapeDtypeStruct((n_tok, d), F32),
        scratch_shapes=[pltpu.SMEM((TOP_K * tc,), jnp.int32), pltpu.VMEM((2, TOP_K * tc * ROW_TILE, LANES), F32),
                        pltpu.SemaphoreType.DMA(()), pltpu.SemaphoreType.DMA((2,))],
        compiler_params=_params(("arbitrary",)),
        name="moe_combine",
    )(dest_t.reshape(n_steps, 1, TOP_K * tc), dest_t.reshape(n_steps, 1, TOP_K * tc), ys, gates_t, x1, mod,
      final_g.reshape(1, d))


def _moe(h2, logits_t, x1, mod, mod_row, layer, ew, final_g, final):
    n_tok = x1.shape[0]
    eidx, gates, rank, cnt = _route(logits_t)
    counts = cnt[:, 0].astype(jnp.int32)
    blk = EXPERT_ROWS
    padded = (counts + blk - 1) // blk * blk
    pad_end = jnp.cumsum(padded)
    pad_start = pad_end - padded
    e4 = eidx[0:TOP_K]
    start_of = jnp.zeros_like(e4)
    for e in range(N_EXPERTS):
        start_of = start_of + jnp.where(e4 == e, pad_start[e], 0)
    dest = start_of + rank[0:TOP_K]
    n_blocks = n_tok * TOP_K // blk + N_EXPERTS
    tok_ids = jnp.broadcast_to(jnp.arange(n_tok, dtype=jnp.int32)[None, :], dest.shape)
    row_tok = jnp.zeros((n_blocks * blk,), jnp.int32).at[dest.reshape(-1)].set(tok_ids.reshape(-1))
    block_row0 = jnp.arange(n_blocks, dtype=jnp.int32) * blk
    block_e = jnp.minimum(jnp.sum((pad_end[None, :] <= block_row0[:, None]).astype(jnp.int32), axis=1),
                          N_EXPERTS - 1)
    ys = _experts(h2, row_tok.reshape(n_blocks, blk), block_e, layer, *ew)
    tc = COMBINE_TILE
    dest_t = dest.reshape(TOP_K, n_tok // tc, tc).transpose(1, 0, 2).reshape(n_tok // tc, TOP_K * tc)
    return _combine(dest_t, ys, gates[0:TOP_K].T, x1, mod, final_g, mod_row, final)


def _layer_params(names, tensors, l):
    return {k: tensors[k][l] for k in names}


def kernel(x_prompt, x_sample, state_rwkv, state_lru, c, c_ctx, norm1_g, norm2_g, final_g, w_mod, b_mod, w_in, w_out,
           rw_mu, rw_w0, rw_w2, rw_a0, rw_a2, rw_g2, rw_kk, rw_ka, rw_rk, rw_lnx_g, rw_lnx_b,
           hy_conv_w, hy_conv_b, hy_f_w1, hy_f_b1, hy_f_w2, hy_f_b2, hy_f_w3, hy_f_b3,
           hy_freq, hy_decay, hy_bias, hy_out_g,
           lru_conv_w, lru_conv_b, lru_wa, lru_ba, lru_wx, lru_bx, lru_lam, lru_out_g,
           router_w, router_b, exp_w_gate, exp_b_gate, exp_w_up, exp_b_up, exp_w_down, exp_b_down):
    tensors = dict(locals())
    bp, sp, d = x_prompt.shape
    bl, sl, _ = x_sample.shape
    depth = w_in.shape[0]
    n_ctx, n_lat = bp * sp, bl * sl
    n_tok = n_ctx + n_lat
    assert sp % SEQ_BLOCK == 0 and sl % SEQ_BLOCK == 0 and n_ctx % sl == 0 and SEQ_BLOCK % GRID_W == 0
    assert n_tok % ROUTE_TILE == 0 and (n_tok * TOP_K) % EXPERT_ROWS == 0

    x = jnp.concatenate([x_prompt.reshape(n_ctx, d), x_sample.reshape(n_lat, d)], axis=0)
    n_rows = 16
    cc = jnp.concatenate([c_ctx[None, :], c, jnp.zeros((n_rows - 1 - bl, d), F32)], axis=0)
    mods = _modulation(cc, w_mod, b_mod).reshape(depth, n_rows, 6, d)
    ctx_tiles = n_ctx // TOK_TILE
    lat_tiles = sl // TOK_TILE
    mod_row = lambda i: jnp.where(i < ctx_tiles, 0, 1 + (i - ctx_tiles) // lat_tiles)

    ew = (exp_w_gate.astype(BF16), exp_b_gate[:, :, None, :], exp_w_up.astype(BF16), exp_b_up[:, :, None, :],
          exp_w_down.astype(BF16), exp_b_down[:, :, None, :])
    zero_rw = jnp.zeros((bp, 2, N_HEADS_A, HEAD, HEAD), F32)
    zero_lru = jnp.zeros((bp, 2, D_C), F32)
    nb_c, nb_l = sp // SEQ_BLOCK, sl // SEQ_BLOCK
    off_l = n_ctx // SEQ_BLOCK
    layer_names = [k for k, t in tensors.items() if k not in ("x_prompt", "x_sample", "state_rwkv", "state_lru", "c",
                                                              "c_ctx", "final_g", "w_mod", "b_mod", "exp_w_gate",
                                                              "exp_b_gate", "exp_w_up", "exp_b_up", "exp_w_down",
                                                              "exp_b_down")]
    new_rw, new_lru = [], []
    for l in range(depth):
        lp = _layer_params(layer_names, tensors, l)
        p = _in_proj(x, mods[l], lp["norm1_g"], lp["w_in"].astype(BF16), mod_row)
        rww, lrw, hyw = _rw_weights(lp), _lru_weights(lp), _hy_weights(lp)
        ya_c, s_rw = _rwkv_mixer(p, rww, zero_rw, 0, bp, nb_c, sp)
        ya_l, _ = _rwkv_mixer(p, rww, state_rwkv[:, l], off_l, bl, nb_l, GRID_W)
        yb_c = _hyena_mixer(p, hyw, 0, bp, sp, sp)
        yb_l = _hyena_mixer(p, hyw, n_ctx // sl, bl, sl, GRID_W)
        hf_c, hb_c, s_lru = _lru_mixer(p, lrw, zero_lru, 0, bp, nb_c, sp)
        hf_l, hb_l, _ = _lru_mixer(p, lrw, state_lru[:, l], off_l, bl, nb_l, GRID_W)
        cat = lambda a, b: jnp.concatenate([a, b], axis=0)
        x1, h2, logits_t = _out_proj(cat(ya_c, ya_l), cat(yb_c, yb_l), cat(hf_c, hf_l), cat(hb_c, hb_l), p, x,
                                     mods[l], lp, mod_row)
        x = _moe(h2, logits_t, x1, mods[l], mod_row, l, ew, final_g, l == depth - 1)
        new_rw.append(s_rw)
        new_lru.append(s_lru)
    y_prompt = x[0:n_ctx].reshape(bp, sp, d)
    y_sample = x[n_ctx:].reshape(bl, sl, d)
    return (y_prompt, y_sample, jnp.stack(new_rw, axis=1), jnp.stack(new_lru, axis=1))
```

```python
import functools
import math

import numpy as np
import jax
import jax.numpy as jnp
from jax import lax
from jax.experimental import pallas as pl
from jax.experimental.pallas import tpu as pltpu

F32 = jnp.float32
BF16 = jnp.bfloat16

GRID_W = 64
HEAD = 64
N_HEADS_A = 6
D_A = N_HEADS_A * HEAD
D_B = 256
D_C = 384
COLS_A = 3 * D_A + 384
COLS_B = 3 * D_B
COLS_C = 2 * D_C
N_EXPERTS = 32
TOP_K = 4
N_BANDS = 16
LRU_C = 8.0
SWIGLU_LIMIT = 7.0
SWIGLU_ALPHA = 1.702
EPS = 1e-6
LNX_EPS = 64e-5

LANES = 128
SEQ_BLOCK = 256
TOK_TILE = 256
CHUNK = 64
SCAN_SEQS = 2
ROUTE_TILE = 512
EXPERT_ROWS = 512
GATHER_UNROLL = 16
DFT_UNROLL = 8
VMEM_LIMIT = 56 * 1024 * 1024

NN = (((1,), (0,)), ((), ()))
NT = (((1,), (1,)), ((), ()))
TN = (((0,), (0,)), ((), ()))


def _dg(a, b, dims=NN):
    return lax.dot_general(a, b, dims, preferred_element_type=F32)


def _dot1(a, b, dims=NN):
    return _dg(a.astype(BF16), b.astype(BF16), dims)


def _split(x):
    hi = x.astype(BF16)
    return hi, (x - hi.astype(F32)).astype(BF16)


def _dot3(a, b, dims=NN):
    ah, al = _split(a)
    bh, bl = _split(b)
    return _dg(ah, bh, dims) + (_dg(ah, bl, dims) + _dg(al, bh, dims))


def _dot_exact_rhs(a, b_bf16, dims=NN):
    a1 = a.astype(BF16)
    r1 = a - a1.astype(F32)
    a2 = r1.astype(BF16)
    a3 = (r1 - a2.astype(F32)).astype(BF16)
    return _dg(a1, b_bf16, dims) + (_dg(a2, b_bf16, dims) + _dg(a3, b_bf16, dims))


def _dot_exact_lhs(a_bf16, b, dims=NN):
    b1 = b.astype(BF16)
    r1 = b - b1.astype(F32)
    b2 = r1.astype(BF16)
    b3 = (r1 - b2.astype(F32)).astype(BF16)
    return _dg(a_bf16, b1, dims) + (_dg(a_bf16, b2, dims) + _dg(a_bf16, b3, dims))


def _iota(shape, axis):
    return lax.broadcasted_iota(jnp.int32, shape, axis)


def _head_ones(n):
    return ((_iota((n, n), 0) // HEAD) == (_iota((n, n), 1) // HEAD)).astype(BF16)


def _sigmoid(x):
    return 1.0 / (1.0 + jnp.exp(-x))


def _params(sem):
    return pltpu.CompilerParams(dimension_semantics=sem, vmem_limit_bytes=VMEM_LIMIT)


def _shifted(u, shift, pos, seg):
    n = u.shape[0]
    rolled = pltpu.roll(u, (-shift) % n, 0)
    ok = (pos + shift >= 0) & (pos + shift < seg)
    return jnp.where(ok, rolled, 0.0)


def _mod_kernel(c_ref, w_ref, b_ref, o_ref):
    c = c_ref[...]
    s = c * _sigmoid(c)
    o_ref[0] = _dot3(s, w_ref[0]) + b_ref[0]


def _modulation(cc, w_mod, b_mod):
    depth, d, n = w_mod.shape
    r = cc.shape[0]
    tn = 1024
    return pl.pallas_call(
        _mod_kernel,
        grid=(depth, n // tn),
        in_specs=[
            pl.BlockSpec((r, d), lambda l, j: (0, 0)),
            pl.BlockSpec((1, d, tn), lambda l, j: (l, 0, j)),
            pl.BlockSpec((1, 1, tn), lambda l, j: (l, 0, j)),
        ],
        out_specs=pl.BlockSpec((1, r, tn), lambda l, j: (l, 0, j)),
        out_shape=jax.ShapeDtypeStruct((depth, r, n), F32),
        compiler_params=_params(("arbitrary", "arbitrary")),
        name="modulation",
    )(cc, w_mod, b_mod.reshape(depth, 1, n))


def _rms(x):
    return x * lax.rsqrt(jnp.mean(x * x, axis=-1, keepdims=True) + EPS)


def _inproj_kernel(x_ref, mod_ref, g_ref, w_ref, o_ref):
    h = _rms(x_ref[...]) * g_ref[...]
    h = h * (1.0 + mod_ref[0, 1:2, :]) + mod_ref[0, 0:1, :]
    o_ref[...] = _dg(h.astype(BF16), w_ref[...])


def _in_proj(x, mod, g, w_bf16, mod_row):
    n_tok, d = x.shape
    n = w_bf16.shape[1]
    return pl.pallas_call(
        _inproj_kernel,
        grid=(n_tok // TOK_TILE,),
        in_specs=[
            pl.BlockSpec((TOK_TILE, d), lambda i: (i, 0)),
            pl.BlockSpec((1, 6, d), lambda i: (mod_row(i), 0, 0)),
            pl.BlockSpec((1, d), lambda i: (0, 0)),
            pl.BlockSpec((d, n), lambda i: (0, 0)),
        ],
        out_specs=pl.BlockSpec((TOK_TILE, n), lambda i: (i, 0)),
        out_shape=jax.ShapeDtypeStruct((n_tok, n), F32),
        compiler_params=_params(("arbitrary",)),
        name="in_proj",
    )(x, mod, g.reshape(1, d), w_bf16)


def _rw_pre_kernel(seg, p_ref, mu_ref, w0_ref, w2_ref, a0_ref, a2_ref, g2_ref, kkw_ref, ka_ref, rk_ref,
                   r_ref, v_ref, kk_ref, kd_ref, bb_ref, lw_ref, bonus_ref, gate_ref):
    p = p_ref[...]
    n = p.shape[0]
    pos = _iota((n, 1), 0) % seg
    ps = p + (0.5 * _shifted(p, -1, pos, seg) + 0.5 * _shifted(p, 1, pos, seg) - p) * mu_ref[...]
    r = ps[:, 0:D_A]
    k = ps[:, D_A:2 * D_A]
    v = ps[:, 2 * D_A:3 * D_A]
    lo = ps[:, 3 * D_A:3 * D_A + 256]
    g_lo = ps[:, 3 * D_A + 256:3 * D_A + 384]
    ones = _head_ones(D_A)
    kk = k * kkw_ref[...]
    kk = kk * lax.rsqrt(_dot_exact_rhs(kk * kk, ones) + 1e-12)
    tanh_wl = jnp.tanh(lo[:, 0:128])
    al = lo[:, 128:256]
    bonus = jnp.zeros_like(r)
    for d in range(2):
        x = w0_ref[d] + _dot3(tanh_wl, w2_ref[d])
        lw_ref[d] = -_sigmoid(x) * math.exp(-0.5)
        asig = _sigmoid(a0_ref[d] + _dot3(al, a2_ref[d]))
        kd = k * (1.0 + (asig - 1.0) * ka_ref[...])
        kd_ref[d] = kd
        bb_ref[d] = kk * asig
        bonus = bonus + _dot_exact_rhs(r * kd * rk_ref[...], ones)
    r_ref[...] = r
    v_ref[...] = v
    kk_ref[...] = kk
    bonus_ref[...] = bonus * v
    gate_ref[...] = _dot3(_sigmoid(g_lo), g2_ref[...])


def _rw_pre(p, lw, off, bsz, nb, seg):
    n = bsz * nb * SEQ_BLOCK
    tok = lambda b, i: (off + b * nb + i, 0)
    out_tok = lambda b, i: (b * nb + i, 0)
    out_tok2 = lambda b, i: (0, b * nb + i, 0)
    full = lambda *shape: pl.BlockSpec(shape, lambda b, i: (0,) * len(shape))
    one = jax.ShapeDtypeStruct((n, D_A), F32)
    two = jax.ShapeDtypeStruct((2, n, D_A), F32)
    s1 = pl.BlockSpec((SEQ_BLOCK, D_A), out_tok)
    s2 = pl.BlockSpec((2, SEQ_BLOCK, D_A), out_tok2)
    return pl.pallas_call(
        functools.partial(_rw_pre_kernel, seg),
        grid=(bsz, nb),
        in_specs=[
            pl.BlockSpec((SEQ_BLOCK, COLS_A), tok),
            full(1, COLS_A), full(2, 1, D_A), full(2, 128, D_A), full(2, 1, D_A), full(2, 128, D_A),
            full(128, D_A), full(1, D_A), full(1, D_A), full(1, D_A),
        ],
        out_specs=[s1, s1, s1, s2, s2, s2, s1, s1],
        out_shape=[one, one, one, two, two, two, one, one],
        compiler_params=_params(("arbitrary", "arbitrary")),
        name="rwkv_pre",
    )(p, lw["mu"], lw["w0"], lw["w2p"], lw["a0"], lw["a2p"], lw["g2"], lw["kk"], lw["ka"], lw["rk"])


_inv_dot = _dot1
_attn_dot = _dot1
_state_dot = _dot1
_dft_dot = _dot1


def _unit_tri_inverses(nmats, eye, r2, c2):
    b16 = (r2 // 16) == (c2 // 16)
    b32 = (r2 // 32) == (c2 // 32)
    pws = [jnp.where(b16, n, 0.0) for n in nmats]
    xs = [eye + p for p in pws]
    for _ in range(3):
        pws = [_inv_dot(p, p) for p in pws]
        xs = [x + _inv_dot(x, p) for x, p in zip(xs, pws)]
    for off_mask in (b32 & ~b16, ~b32):
        ts = [_inv_dot(x, jnp.where(off_mask, n, 0.0)) for x, n in zip(xs, nmats)]
        xs = [x + _inv_dot(t, x) for x, t in zip(xs, ts)]
    return xs


def _rw_scan_kernel(nb, rf_ref, vf_ref, kkf_ref, rb_ref, vb_ref, kkb_ref, kdf_ref, bbf_ref, lwf_ref,
                    kdb_ref, bbb_ref, lwb_ref, s0_ref, yf_ref, yb_ref, sfin_ref, s_scr):
    i = pl.program_id(1)

    @pl.when(i == 0)
    def _():
        s_scr[...] = s0_ref[...]

    c = CHUNK
    nseq = rf_ref.shape[0]
    nsub = rf_ref.shape[1] // c
    n_pairs = N_HEADS_A // 2
    row = _iota((c, c), 0)
    col = _iota((c, c), 1)
    r2 = _iota((2 * c, 2 * c), 0)
    c2 = _iota((2 * c, 2 * c), 1)
    hmask = (r2 // c) == (c2 // c)
    tr = r2 % c
    tc = c2 % c
    eye = (r2 == c2).astype(F32)
    lane_lo = _iota((c, 2 * c), 1) < c
    tri = [(col <= row).astype(BF16), (col >= row).astype(BF16)]
    m_strict = [hmask & (tr > tc), hmask & (tr < tc)]
    m_incl = [hmask & (tr >= tc), hmask & (tr <= tc)]
    dirs = ((rf_ref, vf_ref, kkf_ref, kdf_ref, bbf_ref, lwf_ref, yf_ref),
            (rb_ref, vb_ref, kkb_ref, kdb_ref, bbb_ref, lwb_ref, yb_ref))

    def stack_masked(x):
        return jnp.concatenate([jnp.where(lane_lo, x, 0.0), jnp.where(lane_lo, 0.0, x)], axis=0)

    def twice(x):
        return jnp.concatenate([x, x], axis=0)

    def body(j, carry):
        chains = []
        for q, (d, (r_ref, v_ref, kk_ref, kd_ref, bb_ref, lw_ref, y_ref)) in (
                (q, dd) for q in range(nseq) for dd in enumerate(dirs)):
            rev = d == 1
            sub = (nsub - 1 - j) if rev else j
            sl = pl.ds(pl.multiple_of(sub * c, c), c)
            lw = lw_ref[q, sl, :]
            cl = _dot_exact_lhs(tri[d], lw)
            tot = cl[0:1, :] if rev else cl[c - 1:c, :]
            e_out = jnp.exp(-cl)
            e_rest = jnp.exp(tot - cl)
            rt = r_ref[q, sl, :] * jnp.exp(cl)
            at = -kk_ref[q, sl, :] * jnp.exp(cl - lw)
            bb = bb_ref[q, sl, :]
            kd = kd_ref[q, sl, :]
            v = v_ref[q, sl, :]
            ptot = jnp.exp(tot)
            for pr in range(n_pairs):
                ls = slice(2 * c * pr, 2 * c * pr + 2 * c)
                chains.append(dict(
                    q=q, d=d, pr=pr, sl=sl, ls=ls, y_ref=y_ref, ptot=ptot[:, ls],
                    a_s=stack_masked(at[:, ls]), r_s=stack_masked(rt[:, ls]), v2=stack_masked(v[:, ls]),
                    bk=jnp.concatenate([twice((bb * e_out)[:, ls]), twice((kd * e_out)[:, ls])], axis=0),
                    bkp=jnp.concatenate([twice((bb * e_rest)[:, ls]), twice((kd * e_rest)[:, ls])], axis=0)))
        gs = [_attn_dot(jnp.concatenate([ch["a_s"], ch["r_s"]], axis=0), ch["bk"], NT) for ch in chains]
        tinvs = _unit_tri_inverses([jnp.where(m_strict[ch["d"]], g[0:2 * c, 0:2 * c], 0.0)
                                    for ch, g in zip(chains, gs)], eye, r2, c2)
        ss = [s_scr[ch["q"], ch["d"], ch["pr"]] for ch in chains]
        rhs = [_state_dot(ch["a_s"], s, NT)
               + _attn_dot(jnp.where(m_strict[ch["d"]], g[0:2 * c, 2 * c:4 * c], 0.0), ch["v2"])
               for ch, g, s in zip(chains, gs, ss)]
        ypart = [_state_dot(ch["r_s"], s, NT)
                 + _attn_dot(jnp.where(m_incl[ch["d"]], g[2 * c:4 * c, 2 * c:4 * c], 0.0), ch["v2"])
                 for ch, g, s in zip(chains, gs, ss)]
        us = [_attn_dot(t, x) for t, x in zip(tinvs, rhs)]
        ys = [yp + _attn_dot(jnp.where(m_incl[ch["d"]], g[2 * c:4 * c, 0:2 * c], 0.0), u)
              for ch, g, yp, u in zip(chains, gs, ypart, us)]
        upds = [_state_dot(jnp.concatenate([u, ch["v2"]], axis=0), ch["bkp"], TN) for ch, u in zip(chains, us)]
        for ch, y, s, upd in zip(chains, ys, ss, upds):
            ch["y_ref"][ch["q"], ch["sl"], ch["ls"]] = y[0:c, :] + y[c:2 * c, :]
            s_scr[ch["q"], ch["d"], ch["pr"]] = s * ch["ptot"] + jnp.where(hmask, upd, 0.0)
        return carry

    lax.fori_loop(0, nsub, body, 0)

    @pl.when(i == nb - 1)
    def _():
        sfin_ref[...] = s_scr[...]


def _rw_scan(pre, s0_pairs, bsz, nb):
    r, v, kk, kd, bb, lw = pre
    n = bsz * nb * SEQ_BLOCK
    ns = SCAN_SEQS
    assert bsz % ns == 0
    seq3 = lambda t: t.reshape(bsz, nb * SEQ_BLOCK, D_A)
    seq4 = lambda t: t.reshape(2, bsz, nb * SEQ_BLOCK, D_A)
    fwd = pl.BlockSpec((ns, SEQ_BLOCK, D_A), lambda b, i: (b, i, 0))
    bwd = pl.BlockSpec((ns, SEQ_BLOCK, D_A), lambda b, i: (b, nb - 1 - i, 0))
    fwd2 = pl.BlockSpec((None, ns, SEQ_BLOCK, D_A), lambda b, i: (0, b, i, 0))
    bwd2 = pl.BlockSpec((None, ns, SEQ_BLOCK, D_A), lambda b, i: (1, b, nb - 1 - i, 0))
    sst = pl.BlockSpec((ns, 2, 3, 2 * HEAD, 2 * HEAD), lambda b, i: (b, 0, 0, 0, 0))
    y_shape = jax.ShapeDtypeStruct((bsz, nb * SEQ_BLOCK, D_A), F32)
    r, v, kk = seq3(r), seq3(v), seq3(kk)
    kd, bb, lw = seq4(kd), seq4(bb), seq4(lw)
    yf, yb, sfin = pl.pallas_call(
        functools.partial(_rw_scan_kernel, nb),
        grid=(bsz // ns, nb),
        in_specs=[fwd, fwd, fwd, bwd, bwd, bwd, fwd2, fwd2, fwd2, bwd2, bwd2, bwd2, sst],
        out_specs=[fwd, bwd, sst],
        out_shape=[y_shape, y_shape, jax.ShapeDtypeStruct((bsz, 2, 3, 2 * HEAD, 2 * HEAD), F32)],
        scratch_shapes=[pltpu.VMEM((ns, 2, 3, 2 * HEAD, 2 * HEAD), F32)],
        compiler_params=_params(("arbitrary", "arbitrary")),
        name="rwkv_scan",
    )(r, v, kk, r, v, kk, kd, bb, lw, kd, bb, lw, s0_pairs)
    return yf.reshape(n, D_A), yb.reshape(n, D_A), sfin


def _to_pairs(s):
    lead = s.shape[:-3]
    s = s.reshape(lead + (3, 2, HEAD, HEAD))
    z = jnp.zeros_like(s[..., 0, :, :])
    top = jnp.concatenate([s[..., 0, :, :], z], axis=-1)
    bot = jnp.concatenate([z, s[..., 1, :, :]], axis=-1)
    return jnp.concatenate([top, bot], axis=-2)


def _from_pairs(sp):
    lead = sp.shape[:-3]
    return jnp.stack([sp[..., :HEAD, :HEAD], sp[..., HEAD:, HEAD:]], axis=-3).reshape(lead + (6, HEAD, HEAD))


def _lru_kernel(rev, nb, seg, p_ref, cw_ref, cb_ref, wg_ref, bg_ref, lam_ref, h0_ref, h_ref, hfin_ref, hc_scr):
    i = pl.program_id(1)

    @pl.when(i == 0)
    def _():
        hc_scr[...] = h0_ref[0]

    u_in = p_ref[...]
    n = u_in.shape[0]
    t = _iota((n, 1), 0)
    pos = t % seg
    cw = cw_ref[...]
    xm = (cw[0:1] * _shifted(u_in, -2, pos, seg) + cw[1:2] * _shifted(u_in, -1, pos, seg)
          + cw[2:3] * u_in + cw[3:4] * _shifted(u_in, 1, pos, seg)) + cb_ref[...]
    g = _dot3(xm, wg_ref[...]) + bg_ref[...]
    gr = _sigmoid(g[:, 0:D_C])
    gi = _sigmoid(g[:, D_C:2 * D_C])
    lam = lam_ref[...]
    softplus_neg_lam = jnp.maximum(-lam, 0.0) + jnp.log1p(jnp.exp(-jnp.abs(lam)))
    log_a = -LRU_C * gr * softplus_neg_lam
    a = jnp.exp(log_a)
    u = jnp.sqrt(-jnp.tanh(log_a) * (a * a + 1.0)) * gi * xm
    s = 1
    while s < n:
        if rev:
            ok = t < n - s
            a_sh = jnp.where(ok, pltpu.roll(a, n - s, 0), 1.0)
            u_sh = jnp.where(ok, pltpu.roll(u, n - s, 0), 0.0)
        else:
            ok = t >= s
            a_sh = jnp.where(ok, pltpu.roll(a, s, 0), 1.0)
            u_sh = jnp.where(ok, pltpu.roll(u, s, 0), 0.0)
        u = a * u_sh + u
        a = a * a_sh
        s *= 2
    h = a * hc_scr[...] + u
    h_ref[...] = h
    hc_scr[...] = h[0:1, :] if rev else h[n - 1:n, :]

    @pl.when(i == nb - 1)
    def _():
        hfin_ref[0] = hc_scr[...]


def _lru(p, lw, h0, d, off, bsz, nb, seg):
    rev = d == 1
    n = bsz * nb * SEQ_BLOCK
    blk = (lambda i: nb - 1 - i) if rev else (lambda i: i)
    pcol = (COLS_A + COLS_B) // D_C
    full = lambda *shape: pl.BlockSpec(shape, lambda b, i: (0,) * len(shape))
    dsel = lambda *shape: pl.BlockSpec((None,) + shape, lambda b, i: (d,) + (0,) * len(shape))
    return pl.pallas_call(
        functools.partial(_lru_kernel, rev, nb, seg),
        grid=(bsz, nb),
        in_specs=[
            pl.BlockSpec((SEQ_BLOCK, D_C), lambda b, i: (off + b * nb + blk(i), pcol)),
            full(4, D_C), full(1, D_C), dsel(D_C, 2 * D_C), dsel(1, 2 * D_C), dsel(1, D_C),
            pl.BlockSpec((1, None, 1, D_C), lambda b, i: (b, d, 0, 0)),
        ],
        out_specs=[pl.BlockSpec((SEQ_BLOCK, D_C), lambda b, i: (b * nb + blk(i), 0)),
                   pl.BlockSpec((1, 1, D_C), lambda b, i: (b, 0, 0))],
        out_shape=[jax.ShapeDtypeStruct((n, D_C), F32), jax.ShapeDtypeStruct((bsz, 1, D_C), F32)],
        scratch_shapes=[pltpu.VMEM((1, D_C), F32)],
        compiler_params=_params(("arbitrary", "arbitrary")),
        name="rglru_bwd" if rev else "rglru_fwd",
    )(p, lw["cw"], lw["cb"], lw["wg"], lw["bg"], lw["lam"], h0)


def _rw_post_kernel(yf_ref, yb_ref, bonus_ref, gate_ref, g_ref, b_ref, o_ref):
    y = yf_ref[...] + yb_ref[...]
    ones = _head_ones(D_A)
    mean = _dot_exact_rhs(y, ones) * (1.0 / HEAD)
    yc = y - mean
    var = _dot_exact_rhs(yc * yc, ones) * (1.0 / HEAD)
    yn = yc * lax.rsqrt(var + LNX_EPS) * g_ref[...] + b_ref[...]
    o_ref[...] = (yn + bonus_ref[...]) * gate_ref[...]


def _rw_post(yf, yb, bonus, gate, lw):
    n = yf.shape[0]
    s1 = pl.BlockSpec((TOK_TILE, D_A), lambda i: (i, 0))
    w1 = pl.BlockSpec((1, D_A), lambda i: (0, 0))
    return pl.pallas_call(
        _rw_post_kernel,
        grid=(n // TOK_TILE,),
        in_specs=[s1, s1, s1, s1, w1, w1],
        out_specs=s1,
        out_shape=jax.ShapeDtypeStruct((n, D_A), F32),
        compiler_params=_params(("arbitrary",)),
        name="rwkv_post",
    )(yf, yb, bonus, gate, lw["lnx_g"], lw["lnx_b"])


def _rw_weights(lp):
    def pad_dir(w, slot):
        z = jnp.zeros((2, 128, D_A), F32)
        z = z.at[0, 0:64].set(w[0])
        return z.at[1, 64:128].set(w[1])
    return {
        "mu": lp["rw_mu"].reshape(1, COLS_A),
        "w0": lp["rw_w0"].reshape(2, 1, D_A), "w2p": pad_dir(lp["rw_w2"], 0),
        "a0": lp["rw_a0"].reshape(2, 1, D_A), "a2p": pad_dir(lp["rw_a2"], 1),
        "g2": lp["rw_g2"], "kk": lp["rw_kk"].reshape(1, D_A), "ka": lp["rw_ka"].reshape(1, D_A),
        "rk": lp["rw_rk"].reshape(1, D_A),
        "lnx_g": lp["rw_lnx_g"].reshape(1, D_A), "lnx_b": lp["rw_lnx_b"].reshape(1, D_A),
    }


def _rwkv_mixer(p, lw, s0, off, bsz, nb, seg):
    r, v, kk, kd, bb, lwd, bonus, gate = _rw_pre(p, lw, off, bsz, nb, seg)
    yf, yb, sfin = _rw_scan((r, v, kk, kd, bb, lwd), _to_pairs(s0), bsz, nb)
    return _rw_post(yf, yb, bonus, gate, lw), _from_pairs(sfin)


def _lru_weights(lp):
    def block_diag(w):
        eye = jnp.eye(N_HEADS_A, dtype=F32)
        return jnp.einsum("hij,hg->higj", w, eye).reshape(D_C, D_C)
    wg = jnp.stack([jnp.concatenate([block_diag(lp["lru_wa"][d]), block_diag(lp["lru_wx"][d])], axis=1)
                    for d in range(2)])
    bg = jnp.stack([jnp.concatenate([lp["lru_ba"][d], lp["lru_bx"][d]])[None] for d in range(2)])
    return {"cw": lp["lru_conv_w"], "cb": lp["lru_conv_b"].reshape(1, D_C), "wg": wg, "bg": bg,
            "lam": lp["lru_lam"].reshape(2, 1, D_C)}


def _lru_mixer(p, lw, h0, off, bsz, nb, seg):
    h0 = h0.reshape(bsz, 2, 1, D_C)
    hf, ff = _lru(p, lw, h0, 0, off, bsz, nb, seg)
    hb, fb = _lru(p, lw, h0, 1, off, bsz, nb, seg)
    return hf, hb, jnp.concatenate([ff, fb], axis=1)


def _dft_tables(seq):
    n = 2 * seq
    n2 = 128 if n >= 4096 else 16
    n1 = n // n2
    n1h = n1 // 2
    i1 = np.arange(n1h)[None, None, :]
    k1 = np.arange(n1)[None, :, None]
    i2 = np.arange(n2)[:, None, None]
    ph = 2.0 * np.pi * (i1 * k1 / n1 + i2 * k1 / n)
    f1 = np.concatenate([np.cos(ph), -np.sin(ph)], axis=1)
    g3 = np.concatenate([np.cos(ph), -np.sin(ph)], axis=1).transpose(0, 2, 1) / n
    a = 2.0 * np.pi * np.outer(np.arange(n2), np.arange(n2)) / n2
    f2r, f2i = np.cos(a), -np.sin(a)
    f2 = np.block([[f2r, -f2i], [f2i, f2r]])
    f2c = np.block([[f2r, f2i], [-f2i, f2r]])
    as32 = lambda t: jnp.asarray(t, F32)
    return n1, n2, as32(f1), as32(f2), as32(f2c), as32(g3)


def _dft_stage1(src_ref, dst_ref, f1_ref, n1, n2):
    n1h = n1 // 2

    def body(i, carry):
        xs = src_ref[pl.ds(i, n1h, stride=n2), :]
        a = _dft_dot(f1_ref[i], xs)
        dst_ref[pl.ds(i, n1, stride=2 * n2), :] = a[0:n1]
        dst_ref[pl.ds(n2 + i, n1, stride=2 * n2), :] = a[n1:2 * n1]
        return carry

    lax.fori_loop(0, n2, body, 0, unroll=DFT_UNROLL)


def _hy_filter_kernel(n1, n2, z_ref, w1_ref, b1_ref, w2_ref, b2_ref, w3f_ref, w3b_ref, b3f_ref, b3b_ref,
                      dcf_ref, dcb_ref, freq_ref, f1_ref, f2_ref, o_ref, hf_scr, hb_scr, xf_scr, xb_scr):
    z = z_ref[...]
    h = jnp.sin(freq_ref[0:1, :] * (_dot3(z, w1_ref[...]) + b1_ref[...]))
    h = jnp.sin(freq_ref[1:2, :] * (_dot3(h, w2_ref[...]) + b2_ref[...]))
    t = z[:, 0:1]
    hf_scr[...] = (_dot3(h, w3f_ref[...]) + b3f_ref[...]) * jnp.exp(-t * jnp.abs(dcf_ref[...]))
    hb = (_dot3(h, w3b_ref[...]) + b3b_ref[...]) * jnp.exp(-t * jnp.abs(dcb_ref[...]))
    hb_scr[...] = jnp.where(_iota(hb.shape, 0) == 0, 0.0, hb)
    _dft_stage1(hf_scr, xf_scr, f1_ref, n1, n2)
    _dft_stage1(hb_scr, xb_scr, f1_ref, n1, n2)
    f2 = f2_ref[...]

    def body(k, carry):
        sl = pl.ds(pl.multiple_of(k * 2 * n2, 2 * n2), 2 * n2)
        xf = _dot3(f2, xf_scr[sl, :])
        xb = _dot3(f2, xb_scr[sl, :])
        o_ref[sl, :] = jnp.concatenate([xf[0:n2] + xb[0:n2], xf[n2:2 * n2] - xb[n2:2 * n2]], axis=0)
        return carry

    lax.fori_loop(0, n1, body, 0)


def _hy_filter(lw, seq, tables):
    n1, n2, f1, f2, _, _ = tables
    ct = LANES
    nct = D_B // ct
    full = lambda a: pl.BlockSpec(a.shape, lambda j: (0,) * a.ndim)
    col = lambda rows, shift: pl.BlockSpec((rows, ct), lambda j: (0, shift + j))
    pos = jnp.arange(seq, dtype=F32)
    t = pos[:, None] / (seq - 1)
    bands = jnp.linspace(1e-4, N_BANDS - 1, N_BANDS, dtype=F32)
    ang = (2.0 * math.pi / seq) * pos[:, None] * bands[None, :]
    z = jnp.concatenate([t, jnp.cos(ang), -jnp.sin(ang), jnp.zeros((seq, 64 - 1 - 2 * N_BANDS), F32)], axis=-1)
    args = [z, lw["w1p"], lw["b1"], lw["w2"], lw["b2"]]
    return pl.pallas_call(
        functools.partial(_hy_filter_kernel, n1, n2),
        grid=(nct,),
        in_specs=[full(a) for a in args] + [
            col(64, 0), col(64, nct), col(1, 0), col(1, nct), col(1, 0), col(1, nct),
            full(lw["freq"]), full(f1), full(f2)],
        out_specs=pl.BlockSpec((4 * seq, ct), lambda j: (0, j)),
        out_shape=jax.ShapeDtypeStruct((4 * seq, D_B), F32),
        scratch_shapes=[pltpu.VMEM((seq, ct), F32), pltpu.VMEM((seq, ct), F32),
                        pltpu.VMEM((4 * seq, ct), F32), pltpu.VMEM((4 * seq, ct), F32)],
        compiler_params=_params(("arbitrary",)),
        name="hyena_filter",
    )(*args, lw["w3"], lw["w3"], lw["b3"], lw["b3"], lw["decay"], lw["decay"], lw["freq"], f1, f2)


def _hyena_kernel(n1, n2, seg, pv_ref, px1_ref, px2_ref, cwv_ref, cwx1_ref, cwx2_ref, cbv_ref, cbx1_ref, cbx2_ref,
                  bias_ref, h_ref, f1_ref, f2_ref, f2c_ref, g3_ref, o_ref, z_scr, x_scr, y_scr):
    seq = pv_ref.shape[0]
    pos = _iota((seq, 1), 0) % seg

    def conv(p_ref, w_ref, b_ref):
        p = p_ref[...]
        w = w_ref[...]
        return w[0:1] * _shifted(p, -1, pos, seg) + w[1:2] * p + w[2:3] * _shifted(p, 1, pos, seg) + b_ref[...]

    z_scr[...] = conv(pv_ref, cwv_ref, cbv_ref) * conv(px1_ref, cwx1_ref, cbx1_ref)
    _dft_stage1(z_scr, x_scr, f1_ref, n1, n2)
    f2 = f2_ref[...]
    f2c = f2c_ref[...]

    def mid(k, carry):
        sl = pl.ds(pl.multiple_of(k * 2 * n2, 2 * n2), 2 * n2)
        x = _dft_dot(f2, x_scr[sl, :])
        hh = h_ref[sl, :]
        xr, xi = x[0:n2], x[n2:2 * n2]
        hr, hi = hh[0:n2], hh[n2:2 * n2]
        y = jnp.concatenate([xr * hr - xi * hi, xr * hi + xi * hr], axis=0)
        x_scr[sl, :] = _dft_dot(f2c, y)
        return carry

    lax.fori_loop(0, n1, mid, 0, unroll=DFT_UNROLL)
    n1h = n1 // 2

    def last(i, carry):
        d = jnp.concatenate([x_scr[pl.ds(i, n1, stride=2 * n2), :], x_scr[pl.ds(n2 + i, n1, stride=2 * n2), :]], axis=0)
        y_scr[pl.ds(i, n1h, stride=n2), :] = _dft_dot(g3_ref[i], d)
        return carry

    lax.fori_loop(0, n2, last, 0, unroll=DFT_UNROLL)
    o_ref[...] = conv(px2_ref, cwx2_ref, cbx2_ref) * (y_scr[...] + z_scr[...] * bias_ref[...])


def _hyena(p, lw, hspec, tables, off_seq, bsz, seq, seg):
    n1, n2, f1, f2, f2c, g3 = tables
    ct = LANES
    nct = D_B // ct
    c0 = COLS_A // ct
    pcol = lambda g: pl.BlockSpec((seq, ct), lambda j, b: (off_seq + b, c0 + g * nct + j))
    wcol = lambda rows, g: pl.BlockSpec((rows, ct), lambda j, b: (0, g * nct + j))
    once = lambda a: pl.BlockSpec(a.shape, lambda j, b: (0,) * a.ndim)
    return pl.pallas_call(
        functools.partial(_hyena_kernel, n1, n2, seg),
        grid=(nct, bsz),
        in_specs=[pcol(0), pcol(1), pcol(2), wcol(3, 0), wcol(3, 1), wcol(3, 2), wcol(1, 0), wcol(1, 1), wcol(1, 2),
                  wcol(1, 0), pl.BlockSpec((4 * seq, ct), lambda j, b: (0, j)),
                  once(f1), once(f2), once(f2c), once(g3)],
        out_specs=pl.BlockSpec((seq, ct), lambda j, b: (b, j)),
        out_shape=jax.ShapeDtypeStruct((bsz * seq, D_B), F32),
        scratch_shapes=[pltpu.VMEM((seq, ct), F32), pltpu.VMEM((4 * seq, ct), F32), pltpu.VMEM((seq, ct), F32)],
        compiler_params=_params(("arbitrary", "arbitrary")),
        name="hyena",
    )(p, p, p, lw["cw"], lw["cw"], lw["cw"], lw["cb"], lw["cb"], lw["cb"], lw["bias"], hspec, f1, f2, f2c, g3)


def _hy_weights(lp):
    pos_dim = 1 + 2 * N_BANDS
    return {
        "w1p": jnp.zeros((64, 64), F32).at[0:pos_dim].set(lp["hy_f_w1"]), "b1": lp["hy_f_b1"].reshape(1, 64),
        "w2": lp["hy_f_w2"], "b2": lp["hy_f_b2"].reshape(1, 64),
        "w3": lp["hy_f_w3"], "b3": lp["hy_f_b3"].reshape(1, 2 * D_B), "decay": lp["hy_decay"].reshape(1, 2 * D_B),
        "freq": lp["hy_freq"], "cw": lp["hy_conv_w"], "cb": lp["hy_conv_b"].reshape(1, COLS_B),
        "bias": lp["hy_bias"].reshape(1, D_B),
    }


def _hyena_mixer(p, lw, off_seq, bsz, seq, seg):
    tables = _dft_tables(seq)
    hspec = _hy_filter(lw, seq, tables)
    return _hyena(p, lw, hspec, tables, off_seq, bsz, seq, seg)


def _gelu_tanh(x):
    return 0.5 * x * (1.0 + jnp.tanh(math.sqrt(2.0 / math.pi) * (x + 0.044715 * (x * x * x))))


def _outproj_kernel(ya_ref, yb_ref, hf_ref, hb_ref, yg_ref, x_ref, mod_ref, w_ref, gb_ref, gc_ref, g2_ref,
                    rw_ref, rb_ref, x1_ref, h2_ref, lg_ref):
    ybn = _rms(yb_ref[...]) * gb_ref[...]
    ycn = _rms((hf_ref[...] + hb_ref[...]) * _gelu_tanh(yg_ref[...])) * gc_ref[...]
    y = (_dot1(ya_ref[...], w_ref[0:D_A, :]) + _dot1(ybn, w_ref[D_A:D_A + D_B, :])
         + _dot1(ycn, w_ref[D_A + D_B:D_A + D_B + D_C, :]))
    x1 = x_ref[...] + mod_ref[0, 2:3, :] * y
    x1_ref[...] = x1
    h2 = _rms(x1) * g2_ref[...] * (1.0 + mod_ref[0, 4:5, :]) + mod_ref[0, 3:4, :]
    _to_row_tiles(h2_ref, h2)
    lg_ref[...] = _dot3(rw_ref[...], h2, NT) + rb_ref[...]


def _out_proj(ya, yb, hf, hb, p, x, mod, lp, mod_row):
    n_tok, d = x.shape
    tile = lambda w: pl.BlockSpec((TOK_TILE, w), lambda i: (i, 0))
    full = lambda *shape: pl.BlockSpec(shape, lambda i: (0,) * len(shape))
    yg_col = (COLS_A + COLS_B + D_C) // D_C
    return pl.pallas_call(
        _outproj_kernel,
        grid=(n_tok // TOK_TILE,),
        in_specs=[tile(D_A), tile(D_B), tile(D_C), tile(D_C),
                  pl.BlockSpec((TOK_TILE, D_C), lambda i: (i, yg_col)), tile(d),
                  pl.BlockSpec((1, 6, d), lambda i: (mod_row(i), 0, 0)),
                  full(d, d), full(1, D_B), full(1, D_C), full(1, d), full(N_EXPERTS, d), full(N_EXPERTS, 1)],
        out_specs=[tile(d), pl.BlockSpec((TOK_TILE * ROW_TILE, LANES), lambda i: (i, 0)),
                   pl.BlockSpec((N_EXPERTS, TOK_TILE), lambda i: (0, i))],
        out_shape=[jax.ShapeDtypeStruct((n_tok, d), F32), jax.ShapeDtypeStruct((n_tok * ROW_TILE, LANES), F32),
                   jax.ShapeDtypeStruct((N_EXPERTS, n_tok), F32)],
        compiler_params=_params(("arbitrary",)),
        name="out_proj",
    )(ya, yb, hf, hb, p, x, mod, lp["w_out"].astype(BF16), lp["hy_out_g"].reshape(1, D_B),
      lp["lru_out_g"].reshape(1, D_C), lp["norm2_g"].reshape(1, d), lp["router_w"].T,
      lp["router_b"].reshape(N_EXPERTS, 1))


def _route_kernel(lg_ref, eidx_ref, gate_ref, rank_ref, cnt_ref, run_scr):
    i = pl.program_id(0)

    @pl.when(i == 0)
    def _():
        run_scr[...] = jnp.zeros_like(run_scr)

    l = lg_ref[...]
    ne, t = l.shape
    rowi = _iota((ne, t), 0)
    row8 = _iota((8, t), 0)
    tops, hots = [], []
    oh_all = jnp.zeros((ne, t), F32)
    eidx = jnp.zeros((8, t), jnp.int32)
    for j in range(TOP_K):
        m = jnp.max(l, axis=0, keepdims=True)
        idx = jnp.min(jnp.where(l == m, rowi, ne), axis=0, keepdims=True)
        oh = rowi == idx
        tops.append(m)
        hots.append(oh)
        oh_all = oh_all + oh.astype(F32)
        eidx = jnp.where(row8 == j, idx, eidx)
        l = jnp.where(oh, -jnp.inf, l)
    es = [jnp.exp(m - tops[0]) for m in tops]
    denom = es[0] + es[1] + es[2] + es[3]
    before = (_iota((t, t), 0) < _iota((t, t), 1)).astype(BF16)
    cum = _dg(oh_all.astype(BF16), before) + run_scr[...]
    gates = jnp.zeros((8, t), F32)
    rank = jnp.zeros((8, t), jnp.int32)
    for j in range(TOP_K):
        gates = jnp.where(row8 == j, es[j] / denom, gates)
        rj = jnp.sum(jnp.where(hots[j], cum, 0.0), axis=0, keepdims=True)
        rank = jnp.where(row8 == j, rj.astype(jnp.int32), rank)
    eidx_ref[...] = eidx
    gate_ref[...] = gates
    rank_ref[...] = rank
    run = run_scr[...] + jnp.sum(oh_all, axis=1, keepdims=True)
    run_scr[...] = run
    cnt_ref[...] = jnp.broadcast_to(run, cnt_ref.shape)


def _route(logits_t):
    ne, n_tok = logits_t.shape
    t = ROUTE_TILE
    row = lambda dt: jax.ShapeDtypeStruct((8, n_tok), dt)
    spec = pl.BlockSpec((8, t), lambda i: (0, i))
    return pl.pallas_call(
        _route_kernel,
        grid=(n_tok // t,),
        in_specs=[pl.BlockSpec((ne, t), lambda i: (0, i))],
        out_specs=[spec, spec, spec, pl.BlockSpec((ne, LANES), lambda i: (0, 0))],
        out_shape=[row(jnp.int32), row(F32), row(jnp.int32), jax.ShapeDtypeStruct((ne, LANES), F32)],
        scratch_shapes=[pltpu.VMEM((ne, 1), F32)],
        compiler_params=_params(("arbitrary",)),
        name="moe_route",
    )(logits_t)


ROW_TILE = 8


def _to_row_tiles(ref, x):
    n = x.shape[0]
    for s in range(ROW_TILE):
        ref[pl.ds(s, n, stride=ROW_TILE), :] = x[:, s * LANES:(s + 1) * LANES]


def _from_row_tiles(ref, start, n):
    return jnp.concatenate([ref[pl.ds(start * ROW_TILE + s, n, stride=ROW_TILE), :] for s in range(ROW_TILE)],
                           axis=1)


def _row_tile(ref, r):
    return ref.at[pl.ds(pl.multiple_of(r * ROW_TILE, ROW_TILE), ROW_TILE), :]


def _gather_rows(idx_smem, src_hbm, dst, sem, n_rows):
    def body(r, carry):
        pltpu.make_async_copy(_row_tile(src_hbm, idx_smem[r]), _row_tile(dst, r), sem).start()
        return carry
    lax.fori_loop(0, n_rows, body, 0, unroll=GATHER_UNROLL)


def _wait_rows(src_hbm, dst, sem, n_rows):
    def body(r, carry):
        pltpu.make_async_copy(_row_tile(src_hbm, 0), _row_tile(dst, r), sem).wait()
        return carry
    lax.fori_loop(0, n_rows, body, 0, unroll=GATHER_UNROLL)


def _prefetched_gather(i, n_steps, idx_cur, idx_nxt, src_hbm, idx_smem, buf, idx_sem, row_sem, n_rows):
    def issue(idx_ref, slot):
        cp = pltpu.make_async_copy(idx_ref.at[0, 0], idx_smem, idx_sem)
        cp.start()
        cp.wait()
        _gather_rows(idx_smem, src_hbm, buf.at[slot], row_sem.at[slot], n_rows)

    @pl.when(i == 0)
    def _():
        issue(idx_cur, 0)

    @pl.when(i + 1 < n_steps)
    def _():
        issue(idx_nxt, (i + 1) % 2)

    _wait_rows(src_hbm, buf.at[i % 2], row_sem.at[i % 2], n_rows)


def _idx_specs(n_steps, n_rows, index_args):
    cur = pl.BlockSpec((1, 1, n_rows), lambda i, *_: (i, 0, 0))
    nxt = pl.BlockSpec((1, 1, n_rows), lambda i, *_: (jnp.minimum(i + 1, n_steps - 1), 0, 0))
    return [cur, nxt]


def _dispatch_kernel(n_steps, n_fill, fill_steps, idx_ref, h_hbm, xs_hbm, idx_smem, zero_tile, idx_sem, row_sem):
    i = pl.program_id(0)
    tc = COMBINE_TILE
    n_rows = TOP_K * tc
    @pl.when(i == 0)
    def _():
        zero_tile[...] = jnp.zeros_like(zero_tile)

    cp = pltpu.make_async_copy(idx_ref.at[0, 0], idx_smem, idx_sem)
    cp.start()
    cp.wait()

    sem = row_sem.at[i % 2]
    src = h_hbm.at[pl.ds(pl.multiple_of(i * (tc * ROW_TILE), tc * ROW_TILE), tc * ROW_TILE), :]
    for j in range(TOP_K):
        def start(t, carry):
            pltpu.make_async_copy(_row_tile(src, t), _row_tile(xs_hbm, idx_smem[j * tc + t]), sem).start()
            return carry

        lax.fori_loop(0, tc, start, 0, unroll=GATHER_UNROLL)

    @pl.when(i < fill_steps)
    def _():
        def fill(t, carry):
            pltpu.make_async_copy(zero_tile, _row_tile(xs_hbm, idx_smem[n_rows + t]), sem).start()
            return carry

        lax.fori_loop(0, n_fill, fill, 0, unroll=GATHER_UNROLL)

    def wait_step(step):
        def wait_copy(r, carry):
            pltpu.make_async_copy(_row_tile(h_hbm, 0), _row_tile(xs_hbm, 0), row_sem.at[step % 2]).wait()
            return carry

        def wait_fill(r, carry):
            pltpu.make_async_copy(zero_tile, _row_tile(xs_hbm, 0), row_sem.at[step % 2]).wait()
            return carry

        lax.fori_loop(0, n_rows, wait_copy, 0, unroll=GATHER_UNROLL)

        @pl.when(step < fill_steps)
        def _():
            lax.fori_loop(0, n_fill, wait_fill, 0, unroll=GATHER_UNROLL)

    @pl.when(i > 0)
    def _():
        wait_step(i - 1)

    @pl.when(i == n_steps - 1)
    def _():
        wait_step(i)


def _dispatch(h2, dest_t, pad_rows, n_rows_total):
    n_steps = dest_t.shape[0]
    tc = COMBINE_TILE
    n_pad = pad_rows.shape[0]
    n_fill = GATHER_UNROLL
    while n_fill * n_steps < n_pad:
        n_fill *= 2
    assert n_pad % n_fill == 0
    fill_steps = n_pad // n_fill
    fill_idx = jnp.concatenate([pad_rows.astype(jnp.int32).reshape(fill_steps, n_fill),
                                jnp.zeros((n_steps - fill_steps, n_fill), jnp.int32)], axis=0)
    idx = jnp.concatenate([dest_t, fill_idx], axis=1).reshape(n_steps, 1, TOP_K * tc + n_fill)
    return pl.pallas_call(
        functools.partial(_dispatch_kernel, n_steps, n_fill, fill_steps),
        grid=(n_steps,),
        in_specs=[pl.BlockSpec((1, 1, TOP_K * tc + n_fill), lambda i: (i, 0, 0)),
                  pl.BlockSpec(memory_space=pl.ANY)],
        out_specs=pl.BlockSpec(memory_space=pl.ANY),
        out_shape=jax.ShapeDtypeStruct((n_rows_total * ROW_TILE, LANES), F32),
        scratch_shapes=[pltpu.SMEM((TOP_K * tc + n_fill,), jnp.int32), pltpu.VMEM((ROW_TILE, LANES), F32),
                        pltpu.SemaphoreType.DMA(()), pltpu.SemaphoreType.DMA((2,))],
        compiler_params=_params(("arbitrary",)),
        name="moe_dispatch",
    )(idx, h2)


def _expert_kernel(be_ref, nv_ref, x_ref, wg_ref, bg_ref, wu_ref, bu_ref, wd_ref, bd_ref, o_ref):
    i = pl.program_id(0)
    n_valid = nv_ref[i]

    @pl.when(n_valid > 0)
    def _():
        x = _from_row_tiles(x_ref, 0, EXPERT_ROWS).astype(BF16)
        gt = jnp.minimum(_dg(x, wg_ref[...]) + bg_ref[...], SWIGLU_LIMIT)
        up = jnp.clip(_dg(x, wu_ref[...]) + bu_ref[...], -SWIGLU_LIMIT, SWIGLU_LIMIT)
        act = (up + 1.0) * gt * _sigmoid(SWIGLU_ALPHA * gt)
        _to_row_tiles(o_ref, _dg(act.astype(BF16), wd_ref[...]) + bd_ref[...])

    @pl.when(n_valid == 0)
    def _():
        o_ref[...] = jnp.zeros_like(o_ref)


def _experts(xs, block_e, block_valid, layer, wg, bg, wu, bu, wd, bd):
    n_blocks = block_e.shape[0]
    d = ROW_TILE * LANES
    ff = wg.shape[-1]
    wspec = lambda a, b: pl.BlockSpec((None, None, a, b), lambda i, be, nv: (layer, be[i], 0, 0))
    rows = pl.BlockSpec((EXPERT_ROWS * ROW_TILE, LANES), lambda i, be, nv: (i, 0))
    return pl.pallas_call(
        _expert_kernel,
        grid_spec=pltpu.PrefetchScalarGridSpec(
            num_scalar_prefetch=2,
            grid=(n_blocks,),
            in_specs=[rows, wspec(d, ff), wspec(1, ff), wspec(d, ff), wspec(1, ff), wspec(ff, d), wspec(1, d)],
            out_specs=rows,
        ),
        out_shape=jax.ShapeDtypeStruct((n_blocks * EXPERT_ROWS * ROW_TILE, LANES), F32),
        compiler_params=_params(("arbitrary",)),
        name="moe_experts",
    )(block_e, block_valid, xs, wg, bg, wu, bu, wd, bd)


COMBINE_TILE = 128


def _combine_kernel(n_steps, final, idx_cur, idx_nxt, ys_hbm, gate_ref, x1_ref, mod_ref, fg_ref, o_ref,
                    idx_smem, buf, idx_sem, row_sem):
    i = pl.program_id(0)
    tc = COMBINE_TILE
    _prefetched_gather(i, n_steps, idx_cur, idx_nxt, ys_hbm, idx_smem, buf, idx_sem, row_sem, TOP_K * tc)
    rows = buf.at[i % 2]
    g = gate_ref[...]
    moe = g[:, 0:1] * _from_row_tiles(rows, 0, tc)
    for j in range(1, TOP_K):
        moe = moe + g[:, j:j + 1] * _from_row_tiles(rows, j * tc, tc)
    x2 = x1_ref[...] + mod_ref[0, 5:6, :] * moe
    o_ref[...] = _rms(x2) * fg_ref[...] if final else x2


def _combine(dest_t, ys, gates_t, x1, mod, final_g, mod_row, final):
    n_tok, d = x1.shape
    tc = COMBINE_TILE
    n_steps = n_tok // tc
    per = TOK_TILE // tc
    return pl.pallas_call(
        functools.partial(_combine_kernel, n_steps, final),
        grid=(n_steps,),
        in_specs=_idx_specs(n_steps, TOP_K * tc, 1) + [
                  pl.BlockSpec(memory_space=pl.ANY),
                  pl.BlockSpec((tc, TOP_K), lambda i: (i, 0)),
                  pl.BlockSpec((tc, d), lambda i: (i, 0)),
                  pl.BlockSpec((1, 6, d), lambda i: (mod_row(i // per), 0, 0)),
                  pl.BlockSpec((1, d), lambda i: (0, 0))],
        out_specs=pl.BlockSpec((tc, d), lambda i: (i, 0)),
        out_shape=jax.ShapeDtypeStruct((n_tok, d), F32),
        scratch_shapes=[pltpu.SMEM((TOP_K * tc,), jnp.int32), pltpu.VMEM((2, TOP_K * tc * ROW_TILE, LANES), F32),
                        pltpu.SemaphoreType.DMA(()), pltpu.SemaphoreType.DMA((2,))],
        compiler_params=_params(("arbitrary",)),
        name="moe_combine",
    )(dest_t.reshape(n_steps, 1, TOP_K * tc), dest_t.reshape(n_steps, 1, TOP_K * tc), ys, gates_t, x1, mod,
      final_g.reshape(1, d))


def _moe(h2, logits_t, x1, mod, mod_row, layer, ew, final_g, final):
    n_tok = x1.shape[0]
    eidx, gates, rank, cnt = _route(logits_t)
    counts = cnt[:, 0].astype(jnp.int32)
    blk = EXPERT_ROWS
    padded = (counts + blk - 1) // blk * blk
    pad_end = jnp.cumsum(padded)
    pad_start = pad_end - padded
    e4 = eidx[0:TOP_K]
    start_of = jnp.zeros_like(e4)
    for e in range(N_EXPERTS):
        start_of = start_of + jnp.where(e4 == e, pad_start[e], 0)
    dest = start_of + rank[0:TOP_K]
    n_blocks = n_tok * TOP_K // blk + N_EXPERTS
    block_row0 = jnp.arange(n_blocks, dtype=jnp.int32) * blk
    block_e = jnp.minimum(jnp.sum((pad_end[None, :] <= block_row0[:, None]).astype(jnp.int32), axis=1),
                          N_EXPERTS - 1)
    block_valid = jnp.clip(pad_start[block_e] + counts[block_e] - block_row0, 0, blk)
    tc = COMBINE_TILE
    dest_t = dest.reshape(TOP_K, n_tok // tc, tc).transpose(1, 0, 2).reshape(n_tok // tc, TOP_K * tc)
    n_rows_total = n_blocks * blk
    n_pad = n_rows_total - n_tok * TOP_K
    tail = padded - counts
    tail_end = jnp.cumsum(tail)
    gap_base = jnp.concatenate([pad_start + counts - (tail_end - tail), (pad_end[-1] - tail_end[-1])[None]])
    p = jnp.arange(n_pad, dtype=jnp.int32)
    gap = jnp.sum((tail_end[None, :] <= p[:, None]).astype(jnp.int32), axis=1)
    pad_rows = p + jnp.sum(jnp.where(gap[:, None] == jnp.arange(N_EXPERTS + 1)[None, :], gap_base[None, :], 0), axis=1)
    xs = _dispatch(h2, dest_t, pad_rows, n_rows_total)
    ys = _experts(xs, block_e, block_valid, layer, *ew)
    return _combine(dest_t, ys, gates[0:TOP_K].T, x1, mod, final_g, mod_row, final)


def _layer_params(names, tensors, l):
    return {k: tensors[k][l] for k in names}


def kernel(x_prompt, x_sample, state_rwkv, state_lru, c, c_ctx, norm1_g, norm2_g, final_g, w_mod, b_mod, w_in, w_out,
           rw_mu, rw_w0, rw_w2, rw_a0, rw_a2, rw_g2, rw_kk, rw_ka, rw_rk, rw_lnx_g, rw_lnx_b,
           hy_conv_w, hy_conv_b, hy_f_w1, hy_f_b1, hy_f_w2, hy_f_b2, hy_f_w3, hy_f_b3,
           hy_freq, hy_decay, hy_bias, hy_out_g,
           lru_conv_w, lru_conv_b, lru_wa, lru_ba, lru_wx, lru_bx, lru_lam, lru_out_g,
           router_w, router_b, exp_w_gate, exp_b_gate, exp_w_up, exp_b_up, exp_w_down, exp_b_down):
    tensors = dict(locals())
    bp, sp, d = x_prompt.shape
    bl, sl, _ = x_sample.shape
    depth = w_in.shape[0]
    n_ctx, n_lat = bp * sp, bl * sl
    n_tok = n_ctx + n_lat
    assert sp % SEQ_BLOCK == 0 and sl % SEQ_BLOCK == 0 and n_ctx % sl == 0 and SEQ_BLOCK % GRID_W == 0
    assert n_tok % ROUTE_TILE == 0 and (n_tok * TOP_K) % EXPERT_ROWS == 0

    x = jnp.concatenate([x_prompt.reshape(n_ctx, d), x_sample.reshape(n_lat, d)], axis=0)
    n_rows = 16
    cc = jnp.concatenate([c_ctx[None, :], c, jnp.zeros((n_rows - 1 - bl, d), F32)], axis=0)
    mods = _modulation(cc, w_mod, b_mod).reshape(depth, n_rows, 6, d)
    ctx_tiles = n_ctx // TOK_TILE
    lat_tiles = sl // TOK_TILE
    mod_row = lambda i: jnp.where(i < ctx_tiles, 0, 1 + (i - ctx_tiles) // lat_tiles)

    ew = (exp_w_gate.astype(BF16), exp_b_gate[:, :, None, :], exp_w_up.astype(BF16), exp_b_up[:, :, None, :],
          exp_w_down.astype(BF16), exp_b_down[:, :, None, :])
    zero_rw = jnp.zeros((bp, 2, N_HEADS_A, HEAD, HEAD), F32)
    zero_lru = jnp.zeros((bp, 2, D_C), F32)
    nb_c, nb_l = sp // SEQ_BLOCK, sl // SEQ_BLOCK
    off_l = n_ctx // SEQ_BLOCK
    layer_names = [k for k, t in tensors.items() if k not in ("x_prompt", "x_sample", "state_rwkv", "state_lru", "c",
                                                              "c_ctx", "final_g", "w_mod", "b_mod", "exp_w_gate",
                                                              "exp_b_gate", "exp_w_up", "exp_b_up", "exp_w_down",
                                                              "exp_b_down")]
    new_rw, new_lru = [], []
    for l in range(depth):
        lp = _layer_params(layer_names, tensors, l)
        p = _in_proj(x, mods[l], lp["norm1_g"], lp["w_in"].astype(BF16), mod_row)
        rww, lrw, hyw = _rw_weights(lp), _lru_weights(lp), _hy_weights(lp)
        ya_c, s_rw = _rwkv_mixer(p, rww, zero_rw, 0, bp, nb_c, sp)
        ya_l, _ = _rwkv_mixer(p, rww, state_rwkv[:, l], off_l, bl, nb_l, GRID_W)
        yb_c = _hyena_mixer(p, hyw, 0, bp, sp, sp)
        yb_l = _hyena_mixer(p, hyw, n_ctx // sl, bl, sl, GRID_W)
        hf_c, hb_c, s_lru = _lru_mixer(p, lrw, zero_lru, 0, bp, nb_c, sp)
        hf_l, hb_l, _ = _lru_mixer(p, lrw, state_lru[:, l], off_l, bl, nb_l, GRID_W)
        cat = lambda a, b: jnp.concatenate([a, b], axis=0)
        x1, h2, logits_t = _out_proj(cat(ya_c, ya_l), cat(yb_c, yb_l), cat(hf_c, hf_l), cat(hb_c, hb_l), p, x,
                                     mods[l], lp, mod_row)
        x = _moe(h2, logits_t, x1, mods[l], mod_row, l, ew, final_g, l == depth - 1)
        new_rw.append(s_rw)
        new_lru.append(s_lru)
    y_prompt = x[0:n_ctx].reshape(bp, sp, d)
    y_sample = x[n_ctx:].reshape(bl, sl, d)
    return (y_prompt, y_sample, jnp.stack(new_rw, axis=1), jnp.stack(new_lru, axis=1))
```

```python
import functools
import math

import numpy as np
import jax
import jax.numpy as jnp
from jax import lax
from jax.experimental import pallas as pl
from jax.experimental.pallas import tpu as pltpu

F32 = jnp.float32
BF16 = jnp.bfloat16

GRID_W = 64
HEAD = 64
N_HEADS_A = 6
D_A = N_HEADS_A * HEAD
D_B = 256
D_C = 384
COLS_A = 3 * D_A + 384
COLS_B = 3 * D_B
COLS_C = 2 * D_C
N_EXPERTS = 32
TOP_K = 4
N_BANDS = 16
LRU_C = 8.0
SWIGLU_LIMIT = 7.0
SWIGLU_ALPHA = 1.702
EPS = 1e-6
LNX_EPS = 64e-5

LANES = 128
SEQ_BLOCK = 256
TOK_TILE = 256
CHUNK = 64
SCAN_SEQS = 2
ROUTE_TILE = 512
EXPERT_ROWS = 512
GATHER_UNROLL = 16
DFT_UNROLL = 8
VMEM_LIMIT = 56 * 1024 * 1024

NN = (((1,), (0,)), ((), ()))
NT = (((1,), (1,)), ((), ()))
TN = (((0,), (0,)), ((), ()))


def _dg(a, b, dims=NN):
    return lax.dot_general(a, b, dims, preferred_element_type=F32)


def _dot1(a, b, dims=NN):
    return _dg(a.astype(BF16), b.astype(BF16), dims)


def _split(x):
    hi = x.astype(BF16)
    return hi, (x - hi.astype(F32)).astype(BF16)


def _dot3(a, b, dims=NN):
    ah, al = _split(a)
    bh, bl = _split(b)
    return _dg(ah, bh, dims) + (_dg(ah, bl, dims) + _dg(al, bh, dims))


def _dot_exact_rhs(a, b_bf16, dims=NN):
    a1 = a.astype(BF16)
    r1 = a - a1.astype(F32)
    a2 = r1.astype(BF16)
    a3 = (r1 - a2.astype(F32)).astype(BF16)
    return _dg(a1, b_bf16, dims) + (_dg(a2, b_bf16, dims) + _dg(a3, b_bf16, dims))


def _dot_exact_lhs(a_bf16, b, dims=NN):
    b1 = b.astype(BF16)
    r1 = b - b1.astype(F32)
    b2 = r1.astype(BF16)
    b3 = (r1 - b2.astype(F32)).astype(BF16)
    return _dg(a_bf16, b1, dims) + (_dg(a_bf16, b2, dims) + _dg(a_bf16, b3, dims))


def _iota(shape, axis):
    return lax.broadcasted_iota(jnp.int32, shape, axis)


def _head_ones(n):
    return ((_iota((n, n), 0) // HEAD) == (_iota((n, n), 1) // HEAD)).astype(BF16)


def _sigmoid(x):
    return 1.0 / (1.0 + jnp.exp(-x))


def _params(sem):
    return pltpu.CompilerParams(dimension_semantics=sem, vmem_limit_bytes=VMEM_LIMIT)


def _shifted(u, shift, pos, seg):
    n = u.shape[0]
    rolled = pltpu.roll(u, (-shift) % n, 0)
    ok = (pos + shift >= 0) & (pos + shift < seg)
    return jnp.where(ok, rolled, 0.0)


def _mod_kernel(c_ref, w_ref, b_ref, o_ref):
    c = c_ref[...]
    s = c * _sigmoid(c)
    o_ref[0] = _dot3(s, w_ref[0]) + b_ref[0]


def _modulation(cc, w_mod, b_mod):
    depth, d, n = w_mod.shape
    r = cc.shape[0]
    tn = 1024
    return pl.pallas_call(
        _mod_kernel,
        grid=(depth, n // tn),
        in_specs=[
            pl.BlockSpec((r, d), lambda l, j: (0, 0)),
            pl.BlockSpec((1, d, tn), lambda l, j: (l, 0, j)),
            pl.BlockSpec((1, 1, tn), lambda l, j: (l, 0, j)),
        ],
        out_specs=pl.BlockSpec((1, r, tn), lambda l, j: (l, 0, j)),
        out_shape=jax.ShapeDtypeStruct((depth, r, n), F32),
        compiler_params=_params(("arbitrary", "arbitrary")),
        name="modulation",
    )(cc, w_mod, b_mod.reshape(depth, 1, n))


def _rms(x):
    return x * lax.rsqrt(jnp.mean(x * x, axis=-1, keepdims=True) + EPS)


def _inproj_kernel(x_ref, mod_ref, g_ref, w_ref, o_ref):
    h = _rms(x_ref[...]) * g_ref[...]
    h = h * (1.0 + mod_ref[0, 1:2, :]) + mod_ref[0, 0:1, :]
    o_ref[...] = _dg(h.astype(BF16), w_ref[...])


def _in_proj(x, mod, g, w_bf16, mod_row):
    n_tok, d = x.shape
    n = w_bf16.shape[1]
    return pl.pallas_call(
        _inproj_kernel,
        grid=(n_tok // TOK_TILE,),
        in_specs=[
            pl.BlockSpec((TOK_TILE, d), lambda i: (i, 0)),
            pl.BlockSpec((1, 6, d), lambda i: (mod_row(i), 0, 0)),
            pl.BlockSpec((1, d), lambda i: (0, 0)),
            pl.BlockSpec((d, n), lambda i: (0, 0)),
        ],
        out_specs=pl.BlockSpec((TOK_TILE, n), lambda i: (i, 0)),
        out_shape=jax.ShapeDtypeStruct((n_tok, n), F32),
        compiler_params=_params(("arbitrary",)),
        name="in_proj",
    )(x, mod, g.reshape(1, d), w_bf16)


def _rw_pre_kernel(seg, p_ref, mu_ref, w0_ref, w2_ref, a0_ref, a2_ref, g2_ref, kkw_ref, ka_ref, rk_ref,
                   r_ref, v_ref, kk_ref, kd_ref, bb_ref, lw_ref, bonus_ref, gate_ref):
    p = p_ref[...]
    n = p.shape[0]
    pos = _iota((n, 1), 0) % seg
    ps = p + (0.5 * _shifted(p, -1, pos, seg) + 0.5 * _shifted(p, 1, pos, seg) - p) * mu_ref[...]
    r = ps[:, 0:D_A]
    k = ps[:, D_A:2 * D_A]
    v = ps[:, 2 * D_A:3 * D_A]
    lo = ps[:, 3 * D_A:3 * D_A + 256]
    g_lo = ps[:, 3 * D_A + 256:3 * D_A + 384]
    ones = _head_ones(D_A)
    kk = k * kkw_ref[...]
    kk = kk * lax.rsqrt(_dot_exact_rhs(kk * kk, ones) + 1e-12)
    tanh_wl = jnp.tanh(lo[:, 0:128])
    al = lo[:, 128:256]
    bonus = jnp.zeros_like(r)
    for d in range(2):
        x = w0_ref[d] + _dot3(tanh_wl, w2_ref[d])
        lw_ref[d] = -_sigmoid(x) * math.exp(-0.5)
        asig = _sigmoid(a0_ref[d] + _dot3(al, a2_ref[d]))
        kd = k * (1.0 + (asig - 1.0) * ka_ref[...])
        kd_ref[d] = kd
        bb_ref[d] = kk * asig
        bonus = bonus + _dot_exact_rhs(r * kd * rk_ref[...], ones)
    r_ref[...] = r
    v_ref[...] = v
    kk_ref[...] = kk
    bonus_ref[...] = bonus * v
    gate_ref[...] = _dot3(_sigmoid(g_lo), g2_ref[...])


def _rw_pre(p, lw, off, bsz, nb, seg):
    n = bsz * nb * SEQ_BLOCK
    tok = lambda b, i: (off + b * nb + i, 0)
    out_tok = lambda b, i: (b * nb + i, 0)
    out_tok2 = lambda b, i: (0, b * nb + i, 0)
    full = lambda *shape: pl.BlockSpec(shape, lambda b, i: (0,) * len(shape))
    one = jax.ShapeDtypeStruct((n, D_A), F32)
    two = jax.ShapeDtypeStruct((2, n, D_A), F32)
    s1 = pl.BlockSpec((SEQ_BLOCK, D_A), out_tok)
    s2 = pl.BlockSpec((2, SEQ_BLOCK, D_A), out_tok2)
    return pl.pallas_call(
        functools.partial(_rw_pre_kernel, seg),
        grid=(bsz, nb),
        in_specs=[
            pl.BlockSpec((SEQ_BLOCK, COLS_A), tok),
            full(1, COLS_A), full(2, 1, D_A), full(2, 128, D_A), full(2, 1, D_A), full(2, 128, D_A),
            full(128, D_A), full(1, D_A), full(1, D_A), full(1, D_A),
        ],
        out_specs=[s1, s1, s1, s2, s2, s2, s1, s1],
        out_shape=[one, one, one, two, two, two, one, one],
        compiler_params=_params(("arbitrary", "arbitrary")),
        name="rwkv_pre",
    )(p, lw["mu"], lw["w0"], lw["w2p"], lw["a0"], lw["a2p"], lw["g2"], lw["kk"], lw["ka"], lw["rk"])


_inv_dot = _dot1
_attn_dot = _dot1
_state_dot = _dot1
_dft_dot = _dot1


def _unit_tri_inverses(nmats, eye, r2, c2):
    b16 = (r2 // 16) == (c2 // 16)
    b32 = (r2 // 32) == (c2 // 32)
    pws = [jnp.where(b16, n, 0.0) for n in nmats]
    xs = [eye + p for p in pws]
    for _ in range(3):
        pws = [_inv_dot(p, p) for p in pws]
        xs = [x + _inv_dot(x, p) for x, p in zip(xs, pws)]
    for off_mask in (b32 & ~b16, ~b32):
        ts = [_inv_dot(x, jnp.where(off_mask, n, 0.0)) for x, n in zip(xs, nmats)]
        xs = [x + _inv_dot(t, x) for x, t in zip(xs, ts)]
    return xs


def _rw_scan_kernel(nb, rf_ref, vf_ref, kkf_ref, rb_ref, vb_ref, kkb_ref, kdf_ref, bbf_ref, lwf_ref,
                    kdb_ref, bbb_ref, lwb_ref, s0_ref, yf_ref, yb_ref, sfin_ref, s_scr):
    i = pl.program_id(1)

    @pl.when(i == 0)
    def _():
        s_scr[...] = s0_ref[...]

    c = CHUNK
    nseq = rf_ref.shape[0]
    nsub = rf_ref.shape[1] // c
    n_pairs = N_HEADS_A // 2
    row = _iota((c, c), 0)
    col = _iota((c, c), 1)
    r2 = _iota((2 * c, 2 * c), 0)
    c2 = _iota((2 * c, 2 * c), 1)
    hmask = (r2 // c) == (c2 // c)
    tr = r2 % c
    tc = c2 % c
    eye = (r2 == c2).astype(F32)
    lane_lo = _iota((c, 2 * c), 1) < c
    tri = [(col <= row).astype(BF16), (col >= row).astype(BF16)]
    m_strict = [hmask & (tr > tc), hmask & (tr < tc)]
    m_incl = [hmask & (tr >= tc), hmask & (tr <= tc)]
    dirs = ((rf_ref, vf_ref, kkf_ref, kdf_ref, bbf_ref, lwf_ref, yf_ref),
            (rb_ref, vb_ref, kkb_ref, kdb_ref, bbb_ref, lwb_ref, yb_ref))

    def stack_masked(x):
        return jnp.concatenate([jnp.where(lane_lo, x, 0.0), jnp.where(lane_lo, 0.0, x)], axis=0)

    def twice(x):
        return jnp.concatenate([x, x], axis=0)

    def body(j, carry):
        chains = []
        for q, (d, (r_ref, v_ref, kk_ref, kd_ref, bb_ref, lw_ref, y_ref)) in (
                (q, dd) for q in range(nseq) for dd in enumerate(dirs)):
            rev = d == 1
            sub = (nsub - 1 - j) if rev else j
            sl = pl.ds(pl.multiple_of(sub * c, c), c)
            lw = lw_ref[q, sl, :]
            cl = _dot_exact_lhs(tri[d], lw)
            tot = cl[0:1, :] if rev else cl[c - 1:c, :]
            e_out = jnp.exp(-cl)
            e_rest = jnp.exp(tot - cl)
            rt = r_ref[q, sl, :] * jnp.exp(cl)
            at = -kk_ref[q, sl, :] * jnp.exp(cl - lw)
            bb = bb_ref[q, sl, :]
            kd = kd_ref[q, sl, :]
            v = v_ref[q, sl, :]
            ptot = jnp.exp(tot)
            for pr in range(n_pairs):
                ls = slice(2 * c * pr, 2 * c * pr + 2 * c)
                chains.append(dict(
                    q=q, d=d, pr=pr, sl=sl, ls=ls, y_ref=y_ref, ptot=ptot[:, ls],
                    a_s=stack_masked(at[:, ls]), r_s=stack_masked(rt[:, ls]), v2=stack_masked(v[:, ls]),
                    bk=jnp.concatenate([twice((bb * e_out)[:, ls]), twice((kd * e_out)[:, ls])], axis=0),
                    bkp=jnp.concatenate([twice((bb * e_rest)[:, ls]), twice((kd * e_rest)[:, ls])], axis=0)))
        gs = [_attn_dot(jnp.concatenate([ch["a_s"], ch["r_s"]], axis=0), ch["bk"], NT) for ch in chains]
        tinvs = _unit_tri_inverses([jnp.where(m_strict[ch["d"]], g[0:2 * c, 0:2 * c], 0.0)
                                    for ch, g in zip(chains, gs)], eye, r2, c2)
        ss = [s_scr[ch["q"], ch["d"], ch["pr"]] for ch in chains]
        rhs = [_state_dot(ch["a_s"], s, NT)
               + _attn_dot(jnp.where(m_strict[ch["d"]], g[0:2 * c, 2 * c:4 * c], 0.0), ch["v2"])
               for ch, g, s in zip(chains, gs, ss)]
        ypart = [_state_dot(ch["r_s"], s, NT)
                 + _attn_dot(jnp.where(m_incl[ch["d"]], g[2 * c:4 * c, 2 * c:4 * c], 0.0), ch["v2"])
                 for ch, g, s in zip(chains, gs, ss)]
        us = [_attn_dot(t, x) for t, x in zip(tinvs, rhs)]
        ys = [yp + _attn_dot(jnp.where(m_incl[ch["d"]], g[2 * c:4 * c, 0:2 * c], 0.0), u)
              for ch, g, yp, u in zip(chains, gs, ypart, us)]
        upds = [_state_dot(jnp.concatenate([u, ch["v2"]], axis=0), ch["bkp"], TN) for ch, u in zip(chains, us)]
        for ch, y, s, upd in zip(chains, ys, ss, upds):
            ch["y_ref"][ch["q"], ch["sl"], ch["ls"]] = y[0:c, :] + y[c:2 * c, :]
            s_scr[ch["q"], ch["d"], ch["pr"]] = s * ch["ptot"] + jnp.where(hmask, upd, 0.0)
        return carry

    lax.fori_loop(0, nsub, body, 0)

    @pl.when(i == nb - 1)
    def _():
        sfin_ref[...] = s_scr[...]


def _rw_scan(pre, s0_pairs, bsz, nb):
    r, v, kk, kd, bb, lw = pre
    n = bsz * nb * SEQ_BLOCK
    ns = SCAN_SEQS
    assert bsz % ns == 0
    seq3 = lambda t: t.reshape(bsz, nb * SEQ_BLOCK, D_A)
    seq4 = lambda t: t.reshape(2, bsz, nb * SEQ_BLOCK, D_A)
    fwd = pl.BlockSpec((ns, SEQ_BLOCK, D_A), lambda b, i: (b, i, 0))
    bwd = pl.BlockSpec((ns, SEQ_BLOCK, D_A), lambda b, i: (b, nb - 1 - i, 0))
    fwd2 = pl.BlockSpec((None, ns, SEQ_BLOCK, D_A), lambda b, i: (0, b, i, 0))
    bwd2 = pl.BlockSpec((None, ns, SEQ_BLOCK, D_A), lambda b, i: (1, b, nb - 1 - i, 0))
    sst = pl.BlockSpec((ns, 2, 3, 2 * HEAD, 2 * HEAD), lambda b, i: (b, 0, 0, 0, 0))
    y_shape = jax.ShapeDtypeStruct((bsz, nb * SEQ_BLOCK, D_A), F32)
    r, v, kk = seq3(r), seq3(v), seq3(kk)
    kd, bb, lw = seq4(kd), seq4(bb), seq4(lw)
    yf, yb, sfin = pl.pallas_call(
        functools.partial(_rw_scan_kernel, nb),
        grid=(bsz // ns, nb),
        in_specs=[fwd, fwd, fwd, bwd, bwd, bwd, fwd2, fwd2, fwd2, bwd2, bwd2, bwd2, sst],
        out_specs=[fwd, bwd, sst],
        out_shape=[y_shape, y_shape, jax.ShapeDtypeStruct((bsz, 2, 3, 2 * HEAD, 2 * HEAD), F32)],
        scratch_shapes=[pltpu.VMEM((ns, 2, 3, 2 * HEAD, 2 * HEAD), F32)],
        compiler_params=_params(("arbitrary", "arbitrary")),
        name="rwkv_scan",
    )(r, v, kk, r, v, kk, kd, bb, lw, kd, bb, lw, s0_pairs)
    return yf.reshape(n, D_A), yb.reshape(n, D_A), sfin


def _to_pairs(s):
    lead = s.shape[:-3]
    s = s.reshape(lead + (3, 2, HEAD, HEAD))
    z = jnp.zeros_like(s[..., 0, :, :])
    top = jnp.concatenate([s[..., 0, :, :], z], axis=-1)
    bot = jnp.concatenate([z, s[..., 1, :, :]], axis=-1)
    return jnp.concatenate([top, bot], axis=-2)


def _from_pairs(sp):
    lead = sp.shape[:-3]
    return jnp.stack([sp[..., :HEAD, :HEAD], sp[..., HEAD:, HEAD:]], axis=-3).reshape(lead + (6, HEAD, HEAD))


def _lru_kernel(rev, nb, seg, p_ref, cw_ref, cb_ref, wg_ref, bg_ref, lam_ref, h0_ref, h_ref, hfin_ref, hc_scr):
    i = pl.program_id(1)

    @pl.when(i == 0)
    def _():
        hc_scr[...] = h0_ref[0]

    u_in = p_ref[...]
    n = u_in.shape[0]
    t = _iota((n, 1), 0)
    pos = t % seg
    cw = cw_ref[...]
    xm = (cw[0:1] * _shifted(u_in, -2, pos, seg) + cw[1:2] * _shifted(u_in, -1, pos, seg)
          + cw[2:3] * u_in + cw[3:4] * _shifted(u_in, 1, pos, seg)) + cb_ref[...]
    g = _dot3(xm, wg_ref[...]) + bg_ref[...]
    gr = _sigmoid(g[:, 0:D_C])
    gi = _sigmoid(g[:, D_C:2 * D_C])
    lam = lam_ref[...]
    softplus_neg_lam = jnp.maximum(-lam, 0.0) + jnp.log1p(jnp.exp(-jnp.abs(lam)))
    log_a = -LRU_C * gr * softplus_neg_lam
    a = jnp.exp(log_a)
    u = jnp.sqrt(-jnp.tanh(log_a) * (a * a + 1.0)) * gi * xm
    s = 1
    while s < n:
        if rev:
            ok = t < n - s
            a_sh = jnp.where(ok, pltpu.roll(a, n - s, 0), 1.0)
            u_sh = jnp.where(ok, pltpu.roll(u, n - s, 0), 0.0)
        else:
            ok = t >= s
            a_sh = jnp.where(ok, pltpu.roll(a, s, 0), 1.0)
            u_sh = jnp.where(ok, pltpu.roll(u, s, 0), 0.0)
        u = a * u_sh + u
        a = a * a_sh
        s *= 2
    h = a * hc_scr[...] + u
    h_ref[...] = h
    hc_scr[...] = h[0:1, :] if rev else h[n - 1:n, :]

    @pl.when(i == nb - 1)
    def _():
        hfin_ref[0] = hc_scr[...]


def _lru(p, lw, h0, d, off, bsz, nb, seg):
    rev = d == 1
    n = bsz * nb * SEQ_BLOCK
    blk = (lambda i: nb - 1 - i) if rev else (lambda i: i)
    pcol = (COLS_A + COLS_B) // D_C
    full = lambda *shape: pl.BlockSpec(shape, lambda b, i: (0,) * len(shape))
    dsel = lambda *shape: pl.BlockSpec((None,) + shape, lambda b, i: (d,) + (0,) * len(shape))
    return pl.pallas_call(
        functools.partial(_lru_kernel, rev, nb, seg),
        grid=(bsz, nb),
        in_specs=[
            pl.BlockSpec((SEQ_BLOCK, D_C), lambda b, i: (off + b * nb + blk(i), pcol)),
            full(4, D_C), full(1, D_C), dsel(D_C, 2 * D_C), dsel(1, 2 * D_C), dsel(1, D_C),
            pl.BlockSpec((1, None, 1, D_C), lambda b, i: (b, d, 0, 0)),
        ],
        out_specs=[pl.BlockSpec((SEQ_BLOCK, D_C), lambda b, i: (b * nb + blk(i), 0)),
                   pl.BlockSpec((1, 1, D_C), lambda b, i: (b, 0, 0))],
        out_shape=[jax.ShapeDtypeStruct((n, D_C), F32), jax.ShapeDtypeStruct((bsz, 1, D_C), F32)],
        scratch_shapes=[pltpu.VMEM((1, D_C), F32)],
        compiler_params=_params(("arbitrary", "arbitrary")),
        name="rglru_bwd" if rev else "rglru_fwd",
    )(p, lw["cw"], lw["cb"], lw["wg"], lw["bg"], lw["lam"], h0)


def _rw_post_kernel(yf_ref, yb_ref, bonus_ref, gate_ref, g_ref, b_ref, o_ref):
    y = yf_ref[...] + yb_ref[...]
    ones = _head_ones(D_A)
    mean = _dot_exact_rhs(y, ones) * (1.0 / HEAD)
    yc = y - mean
    var = _dot_exact_rhs(yc * yc, ones) * (1.0 / HEAD)
    yn = yc * lax.rsqrt(var + LNX_EPS) * g_ref[...] + b_ref[...]
    o_ref[...] = (yn + bonus_ref[...]) * gate_ref[...]


def _rw_post(yf, yb, bonus, gate, lw):
    n = yf.shape[0]
    s1 = pl.BlockSpec((TOK_TILE, D_A), lambda i: (i, 0))
    w1 = pl.BlockSpec((1, D_A), lambda i: (0, 0))
    return pl.pallas_call(
        _rw_post_kernel,
        grid=(n // TOK_TILE,),
        in_specs=[s1, s1, s1, s1, w1, w1],
        out_specs=s1,
        out_shape=jax.ShapeDtypeStruct((n, D_A), F32),
        compiler_params=_params(("arbitrary",)),
        name="rwkv_post",
    )(yf, yb, bonus, gate, lw["lnx_g"], lw["lnx_b"])


def _rw_weights(lp):
    def pad_dir(w, slot):
        z = jnp.zeros((2, 128, D_A), F32)
        z = z.at[0, 0:64].set(w[0])
        return z.at[1, 64:128].set(w[1])
    return {
        "mu": lp["rw_mu"].reshape(1, COLS_A),
        "w0": lp["rw_w0"].reshape(2, 1, D_A), "w2p": pad_dir(lp["rw_w2"], 0),
        "a0": lp["rw_a0"].reshape(2, 1, D_A), "a2p": pad_dir(lp["rw_a2"], 1),
        "g2": lp["rw_g2"], "kk": lp["rw_kk"].reshape(1, D_A), "ka": lp["rw_ka"].reshape(1, D_A),
        "rk": lp["rw_rk"].reshape(1, D_A),
        "lnx_g": lp["rw_lnx_g"].reshape(1, D_A), "lnx_b": lp["rw_lnx_b"].reshape(1, D_A),
    }


def _rwkv_mixer(p, lw, s0, off, bsz, nb, seg):
    r, v, kk, kd, bb, lwd, bonus, gate = _rw_pre(p, lw, off, bsz, nb, seg)
    yf, yb, sfin = _rw_scan((r, v, kk, kd, bb, lwd), _to_pairs(s0), bsz, nb)
    return _rw_post(yf, yb, bonus, gate, lw), _from_pairs(sfin)


def _lru_weights(lp):
    def block_diag(w):
        eye = jnp.eye(N_HEADS_A, dtype=F32)
        return jnp.einsum("hij,hg->higj", w, eye).reshape(D_C, D_C)
    wg = jnp.stack([jnp.concatenate([block_diag(lp["lru_wa"][d]), block_diag(lp["lru_wx"][d])], axis=1)
                    for d in range(2)])
    bg = jnp.stack([jnp.concatenate([lp["lru_ba"][d], lp["lru_bx"][d]])[None] for d in range(2)])
    return {"cw": lp["lru_conv_w"], "cb": lp["lru_conv_b"].reshape(1, D_C), "wg": wg, "bg": bg,
            "lam": lp["lru_lam"].reshape(2, 1, D_C)}


def _lru_mixer(p, lw, h0, off, bsz, nb, seg):
    h0 = h0.reshape(bsz, 2, 1, D_C)
    hf, ff = _lru(p, lw, h0, 0, off, bsz, nb, seg)
    hb, fb = _lru(p, lw, h0, 1, off, bsz, nb, seg)
    return hf, hb, jnp.concatenate([ff, fb], axis=1)


def _dft_tables(seq):
    n = 2 * seq
    n2 = 128 if n >= 4096 else 16
    n1 = n // n2
    n1h = n1 // 2
    i1 = np.arange(n1h)[None, None, :]
    k1 = np.arange(n1)[None, :, None]
    i2 = np.arange(n2)[:, None, None]
    ph = 2.0 * np.pi * (i1 * k1 / n1 + i2 * k1 / n)
    f1 = np.concatenate([np.cos(ph), -np.sin(ph)], axis=1)
    g3 = np.concatenate([np.cos(ph), -np.sin(ph)], axis=1).transpose(0, 2, 1) / n
    a = 2.0 * np.pi * np.outer(np.arange(n2), np.arange(n2)) / n2
    f2r, f2i = np.cos(a), -np.sin(a)
    f2 = np.block([[f2r, -f2i], [f2i, f2r]])
    f2c = np.block([[f2r, f2i], [-f2i, f2r]])
    as32 = lambda t: jnp.asarray(t, F32)
    return n1, n2, as32(f1), as32(f2), as32(f2c), as32(g3)


def _dft_stage1(src_ref, dst_ref, f1_ref, n1, n2):
    n1h = n1 // 2

    def body(i, carry):
        xs = src_ref[pl.ds(i, n1h, stride=n2), :]
        a = _dft_dot(f1_ref[i], xs)
        dst_ref[pl.ds(i, n1, stride=2 * n2), :] = a[0:n1]
        dst_ref[pl.ds(n2 + i, n1, stride=2 * n2), :] = a[n1:2 * n1]
        return carry

    lax.fori_loop(0, n2, body, 0, unroll=DFT_UNROLL)


def _hy_filter_kernel(n1, n2, z_ref, w1_ref, b1_ref, w2_ref, b2_ref, w3f_ref, w3b_ref, b3f_ref, b3b_ref,
                      dcf_ref, dcb_ref, freq_ref, f1_ref, f2_ref, o_ref, hf_scr, hb_scr, xf_scr, xb_scr):
    z = z_ref[...]
    h = jnp.sin(freq_ref[0:1, :] * (_dot3(z, w1_ref[...]) + b1_ref[...]))
    h = jnp.sin(freq_ref[1:2, :] * (_dot3(h, w2_ref[...]) + b2_ref[...]))
    t = z[:, 0:1]
    hf_scr[...] = (_dot3(h, w3f_ref[...]) + b3f_ref[...]) * jnp.exp(-t * jnp.abs(dcf_ref[...]))
    hb = (_dot3(h, w3b_ref[...]) + b3b_ref[...]) * jnp.exp(-t * jnp.abs(dcb_ref[...]))
    hb_scr[...] = jnp.where(_iota(hb.shape, 0) == 0, 0.0, hb)
    _dft_stage1(hf_scr, xf_scr, f1_ref, n1, n2)
    _dft_stage1(hb_scr, xb_scr, f1_ref, n1, n2)
    f2 = f2_ref[...]

    def body(k, carry):
        sl = pl.ds(pl.multiple_of(k * 2 * n2, 2 * n2), 2 * n2)
        xf = _dot3(f2, xf_scr[sl, :])
        xb = _dot3(f2, xb_scr[sl, :])
        o_ref[sl, :] = jnp.concatenate([xf[0:n2] + xb[0:n2], xf[n2:2 * n2] - xb[n2:2 * n2]], axis=0)
        return carry

    lax.fori_loop(0, n1, body, 0)


def _hy_filter(lw, seq, tables):
    n1, n2, f1, f2, _, _ = tables
    ct = LANES
    nct = D_B // ct
    full = lambda a: pl.BlockSpec(a.shape, lambda j: (0,) * a.ndim)
    col = lambda rows, shift: pl.BlockSpec((rows, ct), lambda j: (0, shift + j))
    pos = jnp.arange(seq, dtype=F32)
    t = pos[:, None] / (seq - 1)
    bands = jnp.linspace(1e-4, N_BANDS - 1, N_BANDS, dtype=F32)
    ang = (2.0 * math.pi / seq) * pos[:, None] * bands[None, :]
    z = jnp.concatenate([t, jnp.cos(ang), -jnp.sin(ang), jnp.zeros((seq, 64 - 1 - 2 * N_BANDS), F32)], axis=-1)
    args = [z, lw["w1p"], lw["b1"], lw["w2"], lw["b2"]]
    return pl.pallas_call(
        functools.partial(_hy_filter_kernel, n1, n2),
        grid=(nct,),
        in_specs=[full(a) for a in args] + [
            col(64, 0), col(64, nct), col(1, 0), col(1, nct), col(1, 0), col(1, nct),
            full(lw["freq"]), full(f1), full(f2)],
        out_specs=pl.BlockSpec((4 * seq, ct), lambda j: (0, j)),
        out_shape=jax.ShapeDtypeStruct((4 * seq, D_B), F32),
        scratch_shapes=[pltpu.VMEM((seq, ct), F32), pltpu.VMEM((seq, ct), F32),
                        pltpu.VMEM((4 * seq, ct), F32), pltpu.VMEM((4 * seq, ct), F32)],
        compiler_params=_params(("arbitrary",)),
        name="hyena_filter",
    )(*args, lw["w3"], lw["w3"], lw["b3"], lw["b3"], lw["decay"], lw["decay"], lw["freq"], f1, f2)


def _hyena_kernel(n1, n2, seg, pv_ref, px1_ref, px2_ref, cwv_ref, cwx1_ref, cwx2_ref, cbv_ref, cbx1_ref, cbx2_ref,
                  bias_ref, h_ref, f1_ref, f2_ref, f2c_ref, g3_ref, o_ref, z_scr, x_scr, y_scr):
    seq = pv_ref.shape[0]
    pos = _iota((seq, 1), 0) % seg

    def conv(p_ref, w_ref, b_ref):
        p = p_ref[...]
        w = w_ref[...]
        return w[0:1] * _shifted(p, -1, pos, seg) + w[1:2] * p + w[2:3] * _shifted(p, 1, pos, seg) + b_ref[...]

    z_scr[...] = conv(pv_ref, cwv_ref, cbv_ref) * conv(px1_ref, cwx1_ref, cbx1_ref)
    _dft_stage1(z_scr, x_scr, f1_ref, n1, n2)
    f2 = f2_ref[...]
    f2c = f2c_ref[...]

    def mid(k, carry):
        sl = pl.ds(pl.multiple_of(k * 2 * n2, 2 * n2), 2 * n2)
        x = _dft_dot(f2, x_scr[sl, :])
        hh = h_ref[sl, :]
        xr, xi = x[0:n2], x[n2:2 * n2]
        hr, hi = hh[0:n2], hh[n2:2 * n2]
        y = jnp.concatenate([xr * hr - xi * hi, xr * hi + xi * hr], axis=0)
        x_scr[sl, :] = _dft_dot(f2c, y)
        return carry

    lax.fori_loop(0, n1, mid, 0, unroll=DFT_UNROLL)
    n1h = n1 // 2

    def last(i, carry):
        d = jnp.concatenate([x_scr[pl.ds(i, n1, stride=2 * n2), :], x_scr[pl.ds(n2 + i, n1, stride=2 * n2), :]], axis=0)
        y_scr[pl.ds(i, n1h, stride=n2), :] = _dft_dot(g3_ref[i], d)
        return carry

    lax.fori_loop(0, n2, last, 0, unroll=DFT_UNROLL)
    o_ref[...] = conv(px2_ref, cwx2_ref, cbx2_ref) * (y_scr[...] + z_scr[...] * bias_ref[...])


def _hyena(p, lw, hspec, tables, off_seq, bsz, seq, seg):
    n1, n2, f1, f2, f2c, g3 = tables
    ct = LANES
    nct = D_B // ct
    c0 = COLS_A // ct
    pcol = lambda g: pl.BlockSpec((seq, ct), lambda j, b: (off_seq + b, c0 + g * nct + j))
    wcol = lambda rows, g: pl.BlockSpec((rows, ct), lambda j, b: (0, g * nct + j))
    once = lambda a: pl.BlockSpec(a.shape, lambda j, b: (0,) * a.ndim)
    return pl.pallas_call(
        functools.partial(_hyena_kernel, n1, n2, seg),
        grid=(nct, bsz),
        in_specs=[pcol(0), pcol(1), pcol(2), wcol(3, 0), wcol(3, 1), wcol(3, 2), wcol(1, 0), wcol(1, 1), wcol(1, 2),
                  wcol(1, 0), pl.BlockSpec((4 * seq, ct), lambda j, b: (0, j)),
                  once(f1), once(f2), once(f2c), once(g3)],
        out_specs=pl.BlockSpec((seq, ct), lambda j, b: (b, j)),
        out_shape=jax.ShapeDtypeStruct((bsz * seq, D_B), F32),
        scratch_shapes=[pltpu.VMEM((seq, ct), F32), pltpu.VMEM((4 * seq, ct), F32), pltpu.VMEM((seq, ct), F32)],
        compiler_params=_params(("arbitrary", "arbitrary")),
        name="hyena",
    )(p, p, p, lw["cw"], lw["cw"], lw["cw"], lw["cb"], lw["cb"], lw["cb"], lw["bias"], hspec, f1, f2, f2c, g3)


def _hy_weights(lp):
    pos_dim = 1 + 2 * N_BANDS
    return {
        "w1p": jnp.zeros((64, 64), F32).at[0:pos_dim].set(lp["hy_f_w1"]), "b1": lp["hy_f_b1"].reshape(1, 64),
        "w2": lp["hy_f_w2"], "b2": lp["hy_f_b2"].reshape(1, 64),
        "w3": lp["hy_f_w3"], "b3": lp["hy_f_b3"].reshape(1, 2 * D_B), "decay": lp["hy_decay"].reshape(1, 2 * D_B),
        "freq": lp["hy_freq"], "cw": lp["hy_conv_w"], "cb": lp["hy_conv_b"].reshape(1, COLS_B),
        "bias": lp["hy_bias"].reshape(1, D_B),
    }


def _hyena_mixer(p, lw, off_seq, bsz, seq, seg):
    tables = _dft_tables(seq)
    hspec = _hy_filter(lw, seq, tables)
    return _hyena(p, lw, hspec, tables, off_seq, bsz, seq, seg)


def _gelu_tanh(x):
    return 0.5 * x * (1.0 + jnp.tanh(math.sqrt(2.0 / math.pi) * (x + 0.044715 * (x * x * x))))


def _outproj_kernel(ya_ref, yb_ref, hf_ref, hb_ref, yg_ref, x_ref, mod_ref, w_ref, gb_ref, gc_ref, g2_ref,
                    rw_ref, rb_ref, x1_ref, h2_ref, lg_ref):
    ybn = _rms(yb_ref[...]) * gb_ref[...]
    ycn = _rms((hf_ref[...] + hb_ref[...]) * _gelu_tanh(yg_ref[...])) * gc_ref[...]
    y = (_dot1(ya_ref[...], w_ref[0:D_A, :]) + _dot1(ybn, w_ref[D_A:D_A + D_B, :])
         + _dot1(ycn, w_ref[D_A + D_B:D_A + D_B + D_C, :]))
    x1 = x_ref[...] + mod_ref[0, 2:3, :] * y
    x1_ref[...] = x1
    h2 = _rms(x1) * g2_ref[...] * (1.0 + mod_ref[0, 4:5, :]) + mod_ref[0, 3:4, :]
    _to_row_tiles(h2_ref, h2)
    lg_ref[...] = _dot3(rw_ref[...], h2, NT) + rb_ref[...]


def _out_proj(ya, yb, hf, hb, p, x, mod, lp, mod_row):
    n_tok, d = x.shape
    tile = lambda w: pl.BlockSpec((TOK_TILE, w), lambda i: (i, 0))
    full = lambda *shape: pl.BlockSpec(shape, lambda i: (0,) * len(shape))
    yg_col = (COLS_A + COLS_B + D_C) // D_C
    return pl.pallas_call(
        _outproj_kernel,
        grid=(n_tok // TOK_TILE,),
        in_specs=[tile(D_A), tile(D_B), tile(D_C), tile(D_C),
                  pl.BlockSpec((TOK_TILE, D_C), lambda i: (i, yg_col)), tile(d),
                  pl.BlockSpec((1, 6, d), lambda i: (mod_row(i), 0, 0)),
                  full(d, d), full(1, D_B), full(1, D_C), full(1, d), full(N_EXPERTS, d), full(N_EXPERTS, 1)],
        out_specs=[tile(d), pl.BlockSpec((TOK_TILE * ROW_TILE, LANES), lambda i: (i, 0)),
                   pl.BlockSpec((N_EXPERTS, TOK_TILE), lambda i: (0, i))],
        out_shape=[jax.ShapeDtypeStruct((n_tok, d), F32), jax.ShapeDtypeStruct((n_tok * ROW_TILE, LANES), F32),
                   jax.ShapeDtypeStruct((N_EXPERTS, n_tok), F32)],
        compiler_params=_params(("arbitrary",)),
        name="out_proj",
    )(ya, yb, hf, hb, p, x, mod, lp["w_out"].astype(BF16), lp["hy_out_g"].reshape(1, D_B),
      lp["lru_out_g"].reshape(1, D_C), lp["norm2_g"].reshape(1, d), lp["router_w"].T,
      lp["router_b"].reshape(N_EXPERTS, 1))


def _route_kernel(lg_ref, eidx_ref, gate_ref, rank_ref, cnt_ref, run_scr):
    i = pl.program_id(0)

    @pl.when(i == 0)
    def _():
        run_scr[...] = jnp.zeros_like(run_scr)

    l = lg_ref[...]
    ne, t = l.shape
    rowi = _iota((ne, t), 0)
    row8 = _iota((8, t), 0)
    tops, hots = [], []
    oh_all = jnp.zeros((ne, t), F32)
    eidx = jnp.zeros((8, t), jnp.int32)
    for j in range(TOP_K):
        m = jnp.max(l, axis=0, keepdims=True)
        idx = jnp.min(jnp.where(l == m, rowi, ne), axis=0, keepdims=True)
        oh = rowi == idx
        tops.append(m)
        hots.append(oh)
        oh_all = oh_all + oh.astype(F32)
        eidx = jnp.where(row8 == j, idx, eidx)
        l = jnp.where(oh, -jnp.inf, l)
    es = [jnp.exp(m - tops[0]) for m in tops]
    denom = es[0] + es[1] + es[2] + es[3]
    before = (_iota((t, t), 0) < _iota((t, t), 1)).astype(BF16)
    cum = _dg(oh_all.astype(BF16), before) + run_scr[...]
    gates = jnp.zeros((8, t), F32)
    rank = jnp.zeros((8, t), jnp.int32)
    for j in range(TOP_K):
        gates = jnp.where(row8 == j, es[j] / denom, gates)
        rj = jnp.sum(jnp.where(hots[j], cum, 0.0), axis=0, keepdims=True)
        rank = jnp.where(row8 == j, rj.astype(jnp.int32), rank)
    eidx_ref[...] = eidx
    gate_ref[...] = gates
    rank_ref[...] = rank
    run = run_scr[...] + jnp.sum(oh_all, axis=1, keepdims=True)
    run_scr[...] = run
    cnt_ref[...] = jnp.broadcast_to(run, cnt_ref.shape)


def _route(logits_t):
    ne, n_tok = logits_t.shape
    t = ROUTE_TILE
    row = lambda dt: jax.ShapeDtypeStruct((8, n_tok), dt)
    spec = pl.BlockSpec((8, t), lambda i: (0, i))
    return pl.pallas_call(
        _route_kernel,
        grid=(n_tok // t,),
        in_specs=[pl.BlockSpec((ne, t), lambda i: (0, i))],
        out_specs=[spec, spec, spec, pl.BlockSpec((ne, LANES), lambda i: (0, 0))],
        out_shape=[row(jnp.int32), row(F32), row(jnp.int32), jax.ShapeDtypeStruct((ne, LANES), F32)],
        scratch_shapes=[pltpu.VMEM((ne, 1), F32)],
        compiler_params=_params(("arbitrary",)),
        name="moe_route",
    )(logits_t)


ROW_TILE = 8


def _to_row_tiles(ref, x):
    n = x.shape[0]
    for s in range(ROW_TILE):
        ref[pl.ds(s, n, stride=ROW_TILE), :] = x[:, s * LANES:(s + 1) * LANES]


def _from_row_tiles(ref, start, n):
    return jnp.concatenate([ref[pl.ds(start * ROW_TILE + s, n, stride=ROW_TILE), :] for s in range(ROW_TILE)],
                           axis=1)


def _row_tile(ref, r):
    return ref.at[pl.ds(pl.multiple_of(r * ROW_TILE, ROW_TILE), ROW_TILE), :]


def _gather_rows(idx_smem, src_hbm, dst, sem, n_rows):
    def body(r, carry):
        pltpu.make_async_copy(_row_tile(src_hbm, idx_smem[r]), _row_tile(dst, r), sem).start()
        return carry
    lax.fori_loop(0, n_rows, body, 0, unroll=GATHER_UNROLL)


def _wait_rows(src_hbm, dst, sem, n_rows):
    def body(r, carry):
        pltpu.make_async_copy(_row_tile(src_hbm, 0), _row_tile(dst, r), sem).wait()
        return carry
    lax.fori_loop(0, n_rows, body, 0, unroll=GATHER_UNROLL)


def _prefetched_gather(i, n_steps, idx_cur, idx_nxt, src_hbm, idx_smem, buf, idx_sem, row_sem, n_rows):
    def issue(idx_ref, slot):
        cp = pltpu.make_async_copy(idx_ref.at[0, 0], idx_smem, idx_sem)
        cp.start()
        cp.wait()
        _gather_rows(idx_smem, src_hbm, buf.at[slot], row_sem.at[slot], n_rows)

    @pl.when(i == 0)
    def _():
        issue(idx_cur, 0)

    @pl.when(i + 1 < n_steps)
    def _():
        issue(idx_nxt, (i + 1) % 2)

    _wait_rows(src_hbm, buf.at[i % 2], row_sem.at[i % 2], n_rows)


def _idx_specs(n_steps, n_rows, index_args):
    cur = pl.BlockSpec((1, 1, n_rows), lambda i, *_: (i, 0, 0))
    nxt = pl.BlockSpec((1, 1, n_rows), lambda i, *_: (jnp.minimum(i + 1, n_steps - 1), 0, 0))
    return [cur, nxt]


DISPATCH_TILE = 256


def _dispatch_kernel(n_fill, fill_steps, idx_ref, h_ref, xs_hbm, idx_smem, zero_tile, idx_sem, row_sem):
    i = pl.program_id(0)
    tc = DISPATCH_TILE
    n_rows = TOP_K * tc

    @pl.when(i == 0)
    def _():
        zero_tile[...] = jnp.zeros_like(zero_tile)

    cp = pltpu.make_async_copy(idx_ref.at[0, 0], idx_smem, idx_sem)
    cp.start()
    cp.wait()

    for j in range(TOP_K):
        def start(t, carry):
            pltpu.make_async_copy(_row_tile(h_ref, t), _row_tile(xs_hbm, idx_smem[j * tc + t]), row_sem).start()
            return carry

        lax.fori_loop(0, tc, start, 0, unroll=GATHER_UNROLL)

    def fill_copy(t):
        return pltpu.make_async_copy(zero_tile, _row_tile(xs_hbm, idx_smem[n_rows + t]), row_sem)

    @pl.when(i < fill_steps)
    def _():
        def fill(t, carry):
            fill_copy(t).start()
            return carry

        lax.fori_loop(0, n_fill, fill, 0, unroll=GATHER_UNROLL)

    def wait_copy(r, carry):
        pltpu.make_async_copy(_row_tile(h_ref, 0), _row_tile(xs_hbm, 0), row_sem).wait()
        return carry

    lax.fori_loop(0, n_rows, wait_copy, 0, unroll=GATHER_UNROLL)

    @pl.when(i < fill_steps)
    def _():
        def wait_fill(t, carry):
            fill_copy(0).wait()
            return carry

        lax.fori_loop(0, n_fill, wait_fill, 0, unroll=GATHER_UNROLL)


def _dispatch(h2, dest_t, pad_rows, n_rows_total):
    n_steps = dest_t.shape[0]
    tc = DISPATCH_TILE
    n_pad = pad_rows.shape[0]
    n_fill = GATHER_UNROLL
    while n_fill * n_steps < n_pad:
        n_fill *= 2
    assert n_pad % n_fill == 0
    fill_steps = n_pad // n_fill
    fill_idx = jnp.concatenate([pad_rows.astype(jnp.int32).reshape(fill_steps, n_fill),
                                jnp.zeros((n_steps - fill_steps, n_fill), jnp.int32)], axis=0)
    idx = jnp.concatenate([dest_t, fill_idx], axis=1).reshape(n_steps, 1, TOP_K * tc + n_fill)
    return pl.pallas_call(
        functools.partial(_dispatch_kernel, n_fill, fill_steps),
        grid=(n_steps,),
        in_specs=[pl.BlockSpec((1, 1, TOP_K * tc + n_fill), lambda i: (i, 0, 0)),
                  pl.BlockSpec((tc * ROW_TILE, LANES), lambda i: (i, 0))],
        out_specs=pl.BlockSpec(memory_space=pl.ANY),
        out_shape=jax.ShapeDtypeStruct((n_rows_total * ROW_TILE, LANES), F32),
        scratch_shapes=[pltpu.SMEM((TOP_K * tc + n_fill,), jnp.int32), pltpu.VMEM((ROW_TILE, LANES), F32),
                        pltpu.SemaphoreType.DMA(()), pltpu.SemaphoreType.DMA(())],
        compiler_params=_params(("arbitrary",)),
        name="moe_dispatch",
    )(idx, h2)


def _expert_kernel(be_ref, nv_ref, x_ref, wg_ref, bg_ref, wu_ref, bu_ref, wd_ref, bd_ref, o_ref):
    i = pl.program_id(0)
    n_valid = nv_ref[i]

    @pl.when(n_valid > 0)
    def _():
        x = _from_row_tiles(x_ref, 0, EXPERT_ROWS).astype(BF16)
        gt = jnp.minimum(_dg(x, wg_ref[...]) + bg_ref[...], SWIGLU_LIMIT)
        up = jnp.clip(_dg(x, wu_ref[...]) + bu_ref[...], -SWIGLU_LIMIT, SWIGLU_LIMIT)
        act = (up + 1.0) * gt * _sigmoid(SWIGLU_ALPHA * gt)
        _to_row_tiles(o_ref, _dg(act.astype(BF16), wd_ref[...]) + bd_ref[...])

    @pl.when(n_valid == 0)
    def _():
        o_ref[...] = jnp.zeros_like(o_ref)


def _experts(xs, block_e, block_valid, layer, wg, bg, wu, bu, wd, bd):
    n_blocks = block_e.shape[0]
    d = ROW_TILE * LANES
    ff = wg.shape[-1]
    wspec = lambda a, b: pl.BlockSpec((None, None, a, b), lambda i, be, nv: (layer, be[i], 0, 0))
    rows = pl.BlockSpec((EXPERT_ROWS * ROW_TILE, LANES), lambda i, be, nv: (i, 0))
    return pl.pallas_call(
        _expert_kernel,
        grid_spec=pltpu.PrefetchScalarGridSpec(
            num_scalar_prefetch=2,
            grid=(n_blocks,),
            in_specs=[rows, wspec(d, ff), wspec(1, ff), wspec(d, ff), wspec(1, ff), wspec(ff, d), wspec(1, d)],
            out_specs=rows,
        ),
        out_shape=jax.ShapeDtypeStruct((n_blocks * EXPERT_ROWS * ROW_TILE, LANES), F32),
        compiler_params=_params(("arbitrary",)),
        name="moe_experts",
    )(block_e, block_valid, xs, wg, bg, wu, bu, wd, bd)


COMBINE_TILE = 128


def _combine_kernel(n_steps, final, idx_cur, idx_nxt, ys_hbm, gate_ref, x1_ref, mod_ref, fg_ref, o_ref,
                    idx_smem, buf, idx_sem, row_sem):
    i = pl.program_id(0)
    tc = COMBINE_TILE
    _prefetched_gather(i, n_steps, idx_cur, idx_nxt, ys_hbm, idx_smem, buf, idx_sem, row_sem, TOP_K * tc)
    rows = buf.at[i % 2]
    g = gate_ref[...]
    moe = g[:, 0:1] * _from_row_tiles(rows, 0, tc)
    for j in range(1, TOP_K):
        moe = moe + g[:, j:j + 1] * _from_row_tiles(rows, j * tc, tc)
    x2 = x1_ref[...] + mod_ref[0, 5:6, :] * moe
    o_ref[...] = _rms(x2) * fg_ref[...] if final else x2


def _combine(dest_t, ys, gates_t, x1, mod, final_g, mod_row, final):
    n_tok, d = x1.shape
    tc = COMBINE_TILE
    n_steps = n_tok // tc
    per = TOK_TILE // tc
    return pl.pallas_call(
        functools.partial(_combine_kernel, n_steps, final),
        grid=(n_steps,),
        in_specs=_idx_specs(n_steps, TOP_K * tc, 1) + [
                  pl.BlockSpec(memory_space=pl.ANY),
                  pl.BlockSpec((tc, TOP_K), lambda i: (i, 0)),
                  pl.BlockSpec((tc, d), lambda i: (i, 0)),
                  pl.BlockSpec((1, 6, d), lambda i: (mod_row(i // per), 0, 0)),
                  pl.BlockSpec((1, d), lambda i: (0, 0))],
        out_specs=pl.BlockSpec((tc, d), lambda i: (i, 0)),
        out_shape=jax.ShapeDtypeStruct((n_tok, d), F32),
        scratch_shapes=[pltpu.SMEM((TOP_K * tc,), jnp.int32), pltpu.VMEM((2, TOP_K * tc * ROW_TILE, LANES), F32),
                        pltpu.SemaphoreType.DMA(()), pltpu.SemaphoreType.DMA((2,))],
        compiler_params=_params(("arbitrary",)),
        name="moe_combine",
    )(dest_t.reshape(n_steps, 1, TOP_K * tc), dest_t.reshape(n_steps, 1, TOP_K * tc), ys, gates_t, x1, mod,
      final_g.reshape(1, d))


def _moe(h2, logits_t, x1, mod, mod_row, layer, ew, final_g, final):
    n_tok = x1.shape[0]
    eidx, gates, rank, cnt = _route(logits_t)
    counts = cnt[:, 0].astype(jnp.int32)
    blk = EXPERT_ROWS
    padded = (counts + blk - 1) // blk * blk
    pad_end = jnp.cumsum(padded)
    pad_start = pad_end - padded
    e4 = eidx[0:TOP_K]
    start_of = jnp.zeros_like(e4)
    for e in range(N_EXPERTS):
        start_of = start_of + jnp.where(e4 == e, pad_start[e], 0)
    dest = start_of + rank[0:TOP_K]
    n_blocks = n_tok * TOP_K // blk + N_EXPERTS
    block_row0 = jnp.arange(n_blocks, dtype=jnp.int32) * blk
    block_e = jnp.minimum(jnp.sum((pad_end[None, :] <= block_row0[:, None]).astype(jnp.int32), axis=1),
                          N_EXPERTS - 1)
    block_valid = jnp.clip(pad_start[block_e] + counts[block_e] - block_row0, 0, blk)
    tc = COMBINE_TILE
    dest_t = dest.reshape(TOP_K, n_tok // tc, tc).transpose(1, 0, 2).reshape(n_tok // tc, TOP_K * tc)
    n_rows_total = n_blocks * blk
    n_pad = n_rows_total - n_tok * TOP_K
    tail = padded - counts
    tail_end = jnp.cumsum(tail)
    gap_base = jnp.concatenate([pad_start + counts - (tail_end - tail), (pad_end[-1] - tail_end[-1])[None]])
    p = jnp.arange(n_pad, dtype=jnp.int32)
    gap = jnp.sum((tail_end[None, :] <= p[:, None]).astype(jnp.int32), axis=1)
    pad_rows = p + jnp.sum(jnp.where(gap[:, None] == jnp.arange(N_EXPERTS + 1)[None, :], gap_base[None, :], 0), axis=1)
    td = DISPATCH_TILE
    dest_d = dest.reshape(TOP_K, n_tok // td, td).transpose(1, 0, 2).reshape(n_tok // td, TOP_K * td)
    xs = _dispatch(h2, dest_d, pad_rows, n_rows_total)
    ys = _experts(xs, block_e, block_valid, layer, *ew)
    return _combine(dest_t, ys, gates[0:TOP_K].T, x1, mod, final_g, mod_row, final)


def _layer_params(names, tensors, l):
    return {k: tensors[k][l] for k in names}


def kernel(x_prompt, x_sample, state_rwkv, state_lru, c, c_ctx, norm1_g, norm2_g, final_g, w_mod, b_mod, w_in, w_out,
           rw_mu, rw_w0, rw_w2, rw_a0, rw_a2, rw_g2, rw_kk, rw_ka, rw_rk, rw_lnx_g, rw_lnx_b,
           hy_conv_w, hy_conv_b, hy_f_w1, hy_f_b1, hy_f_w2, hy_f_b2, hy_f_w3, hy_f_b3,
           hy_freq, hy_decay, hy_bias, hy_out_g,
           lru_conv_w, lru_conv_b, lru_wa, lru_ba, lru_wx, lru_bx, lru_lam, lru_out_g,
           router_w, router_b, exp_w_gate, exp_b_gate, exp_w_up, exp_b_up, exp_w_down, exp_b_down):
    tensors = dict(locals())
    bp, sp, d = x_prompt.shape
    bl, sl, _ = x_sample.shape
    depth = w_in.shape[0]
    n_ctx, n_lat = bp * sp, bl * sl
    n_tok = n_ctx + n_lat
    assert sp % SEQ_BLOCK == 0 and sl % SEQ_BLOCK == 0 and n_ctx % sl == 0 and SEQ_BLOCK % GRID_W == 0
    assert n_tok % ROUTE_TILE == 0 and (n_tok * TOP_K) % EXPERT_ROWS == 0

    x = jnp.concatenate([x_prompt.reshape(n_ctx, d), x_sample.reshape(n_lat, d)], axis=0)
    n_rows = 16
    cc = jnp.concatenate([c_ctx[None, :], c, jnp.zeros((n_rows - 1 - bl, d), F32)], axis=0)
    mods = _modulation(cc, w_mod, b_mod).reshape(depth, n_rows, 6, d)
    ctx_tiles = n_ctx // TOK_TILE
    lat_tiles = sl // TOK_TILE
    mod_row = lambda i: jnp.where(i < ctx_tiles, 0, 1 + (i - ctx_tiles) // lat_tiles)

    ew = (exp_w_gate.astype(BF16), exp_b_gate[:, :, None, :], exp_w_up.astype(BF16), exp_b_up[:, :, None, :],
          exp_w_down.astype(BF16), exp_b_down[:, :, None, :])
    zero_rw = jnp.zeros((bp, 2, N_HEADS_A, HEAD, HEAD), F32)
    zero_lru = jnp.zeros((bp, 2, D_C), F32)
    nb_c, nb_l = sp // SEQ_BLOCK, sl // SEQ_BLOCK
    off_l = n_ctx // SEQ_BLOCK
    layer_names = [k for k, t in tensors.items() if k not in ("x_prompt", "x_sample", "state_rwkv", "state_lru", "c",
                                                              "c_ctx", "final_g", "w_mod", "b_mod", "exp_w_gate",
                                                              "exp_b_gate", "exp_w_up", "exp_b_up", "exp_w_down",
                                                              "exp_b_down")]
    new_rw, new_lru = [], []
    for l in range(depth):
        lp = _layer_params(layer_names, tensors, l)
        p = _in_proj(x, mods[l], lp["norm1_g"], lp["w_in"].astype(BF16), mod_row)
        rww, lrw, hyw = _rw_weights(lp), _lru_weights(lp), _hy_weights(lp)
        ya_c, s_rw = _rwkv_mixer(p, rww, zero_rw, 0, bp, nb_c, sp)
        ya_l, _ = _rwkv_mixer(p, rww, state_rwkv[:, l], off_l, bl, nb_l, GRID_W)
        yb_c = _hyena_mixer(p, hyw, 0, bp, sp, sp)
        yb_l = _hyena_mixer(p, hyw, n_ctx // sl, bl, sl, GRID_W)
        hf_c, hb_c, s_lru = _lru_mixer(p, lrw, zero_lru, 0, bp, nb_c, sp)
        hf_l, hb_l, _ = _lru_mixer(p, lrw, state_lru[:, l], off_l, bl, nb_l, GRID_W)
        cat = lambda a, b: jnp.concatenate([a, b], axis=0)
        x1, h2, logits_t = _out_proj(cat(ya_c, ya_l), cat(yb_c, yb_l), cat(hf_c, hf_l), cat(hb_c, hb_l), p, x,
                                     mods[l], lp, mod_row)
        x = _moe(h2, logits_t, x1, mods[l], mod_row, l, ew, final_g, l == depth - 1)
        new_rw.append(s_rw)
        new_lru.append(s_lru)
    y_prompt = x[0:n_ctx].reshape(bp, sp, d)
    y_sample = x[n_ctx:].reshape(bl, sl, d)
    return (y_prompt, y_sample, jnp.stack(new_rw, axis=1), jnp.stack(new_lru, axis=1))
```

```python
import functools
import math

import numpy as np
import jax
import jax.numpy as jnp
from jax import lax
from jax.experimental import pallas as pl
from jax.experimental.pallas import tpu as pltpu

F32 = jnp.float32
BF16 = jnp.bfloat16

GRID_W = 64
HEAD = 64
N_HEADS_A = 6
D_A = N_HEADS_A * HEAD
D_B = 256
D_C = 384
COLS_A = 3 * D_A + 384
COLS_B = 3 * D_B
COLS_C = 2 * D_C
N_EXPERTS = 32
TOP_K = 4
N_BANDS = 16
LRU_C = 8.0
SWIGLU_LIMIT = 7.0
SWIGLU_ALPHA = 1.702
EPS = 1e-6
LNX_EPS = 64e-5

LANES = 128
SEQ_BLOCK = 256
TOK_TILE = 256
CHUNK = 64
SCAN_SEQS = 2
ROUTE_TILE = 512
EXPERT_ROWS = 512
GATHER_UNROLL = 16
DFT_UNROLL = 8
VMEM_LIMIT = 56 * 1024 * 1024

NN = (((1,), (0,)), ((), ()))
NT = (((1,), (1,)), ((), ()))
TN = (((0,), (0,)), ((), ()))


def _dg(a, b, dims=NN):
    return lax.dot_general(a, b, dims, preferred_element_type=F32)


def _dot1(a, b, dims=NN):
    return _dg(a.astype(BF16), b.astype(BF16), dims)


def _split(x):
    hi = x.astype(BF16)
    return hi, (x - hi.astype(F32)).astype(BF16)


def _dot3(a, b, dims=NN):
    ah, al = _split(a)
    bh, bl = _split(b)
    return _dg(ah, bh, dims) + (_dg(ah, bl, dims) + _dg(al, bh, dims))


def _dot_exact_rhs(a, b_bf16, dims=NN):
    a1 = a.astype(BF16)
    r1 = a - a1.astype(F32)
    a2 = r1.astype(BF16)
    a3 = (r1 - a2.astype(F32)).astype(BF16)
    return _dg(a1, b_bf16, dims) + (_dg(a2, b_bf16, dims) + _dg(a3, b_bf16, dims))


def _dot_exact_lhs(a_bf16, b, dims=NN):
    b1 = b.astype(BF16)
    r1 = b - b1.astype(F32)
    b2 = r1.astype(BF16)
    b3 = (r1 - b2.astype(F32)).astype(BF16)
    return _dg(a_bf16, b1, dims) + (_dg(a_bf16, b2, dims) + _dg(a_bf16, b3, dims))


def _iota(shape, axis):
    return lax.broadcasted_iota(jnp.int32, shape, axis)


def _head_ones(n):
    return ((_iota((n, n), 0) // HEAD) == (_iota((n, n), 1) // HEAD)).astype(BF16)


def _sigmoid(x):
    return 1.0 / (1.0 + jnp.exp(-x))


def _params(sem):
    return pltpu.CompilerParams(dimension_semantics=sem, vmem_limit_bytes=VMEM_LIMIT)


def _shifted(u, shift, pos, seg):
    n = u.shape[0]
    rolled = pltpu.roll(u, (-shift) % n, 0)
    ok = (pos + shift >= 0) & (pos + shift < seg)
    return jnp.where(ok, rolled, 0.0)


def _mod_kernel(c_ref, w_ref, b_ref, o_ref):
    c = c_ref[...]
    s = c * _sigmoid(c)
    o_ref[0] = _dot3(s, w_ref[0]) + b_ref[0]


def _modulation(cc, w_mod, b_mod):
    depth, d, n = w_mod.shape
    r = cc.shape[0]
    tn = 1024
    return pl.pallas_call(
        _mod_kernel,
        grid=(depth, n // tn),
        in_specs=[
            pl.BlockSpec((r, d), lambda l, j: (0, 0)),
            pl.BlockSpec((1, d, tn), lambda l, j: (l, 0, j)),
            pl.BlockSpec((1, 1, tn), lambda l, j: (l, 0, j)),
        ],
        out_specs=pl.BlockSpec((1, r, tn), lambda l, j: (l, 0, j)),
        out_shape=jax.ShapeDtypeStruct((depth, r, n), F32),
        compiler_params=_params(("arbitrary", "arbitrary")),
        name="modulation",
    )(cc, w_mod, b_mod.reshape(depth, 1, n))


def _rms(x):
    return x * lax.rsqrt(jnp.mean(x * x, axis=-1, keepdims=True) + EPS)


def _inproj_kernel(x_ref, mod_ref, g_ref, w_ref, o_ref):
    h = _rms(x_ref[...]) * g_ref[...]
    h = h * (1.0 + mod_ref[0, 1:2, :]) + mod_ref[0, 0:1, :]
    o_ref[...] = _dg(h.astype(BF16), w_ref[...])


def _in_proj(x, mod, g, w_bf16, mod_row):
    n_tok, d = x.shape
    n = w_bf16.shape[1]
    return pl.pallas_call(
        _inproj_kernel,
        grid=(n_tok // TOK_TILE,),
        in_specs=[
            pl.BlockSpec((TOK_TILE, d), lambda i: (i, 0)),
            pl.BlockSpec((1, 6, d), lambda i: (mod_row(i), 0, 0)),
            pl.BlockSpec((1, d), lambda i: (0, 0)),
            pl.BlockSpec((d, n), lambda i: (0, 0)),
        ],
        out_specs=pl.BlockSpec((TOK_TILE, n), lambda i: (i, 0)),
        out_shape=jax.ShapeDtypeStruct((n_tok, n), F32),
        compiler_params=_params(("arbitrary",)),
        name="in_proj",
    )(x, mod, g.reshape(1, d), w_bf16)


def _rw_pre_kernel(seg, p_ref, mu_ref, w0_ref, w2_ref, a0_ref, a2_ref, g2_ref, kkw_ref, ka_ref, rk_ref,
                   r_ref, v_ref, kk_ref, kd_ref, bb_ref, lw_ref, bonus_ref, gate_ref):
    p = p_ref[...]
    n = p.shape[0]
    pos = _iota((n, 1), 0) % seg
    ps = p + (0.5 * _shifted(p, -1, pos, seg) + 0.5 * _shifted(p, 1, pos, seg) - p) * mu_ref[...]
    r = ps[:, 0:D_A]
    k = ps[:, D_A:2 * D_A]
    v = ps[:, 2 * D_A:3 * D_A]
    lo = ps[:, 3 * D_A:3 * D_A + 256]
    g_lo = ps[:, 3 * D_A + 256:3 * D_A + 384]
    ones = _head_ones(D_A)
    kk = k * kkw_ref[...]
    kk = kk * lax.rsqrt(_dot_exact_rhs(kk * kk, ones) + 1e-12)
    tanh_wl = jnp.tanh(lo[:, 0:128])
    al = lo[:, 128:256]
    bonus = jnp.zeros_like(r)
    for d in range(2):
        x = w0_ref[d] + _dot3(tanh_wl, w2_ref[d])
        lw_ref[d] = -_sigmoid(x) * math.exp(-0.5)
        asig = _sigmoid(a0_ref[d] + _dot3(al, a2_ref[d]))
        kd = k * (1.0 + (asig - 1.0) * ka_ref[...])
        kd_ref[d] = kd
        bb_ref[d] = kk * asig
        bonus = bonus + _dot_exact_rhs(r * kd * rk_ref[...], ones)
    r_ref[...] = r
    v_ref[...] = v
    kk_ref[...] = kk
    bonus_ref[...] = bonus * v
    gate_ref[...] = _dot3(_sigmoid(g_lo), g2_ref[...])


def _rw_pre(p, lw, off, bsz, nb, seg):
    n = bsz * nb * SEQ_BLOCK
    tok = lambda b, i: (off + b * nb + i, 0)
    out_tok = lambda b, i: (b * nb + i, 0)
    out_tok2 = lambda b, i: (0, b * nb + i, 0)
    full = lambda *shape: pl.BlockSpec(shape, lambda b, i: (0,) * len(shape))
    one = jax.ShapeDtypeStruct((n, D_A), F32)
    two = jax.ShapeDtypeStruct((2, n, D_A), F32)
    s1 = pl.BlockSpec((SEQ_BLOCK, D_A), out_tok)
    s2 = pl.BlockSpec((2, SEQ_BLOCK, D_A), out_tok2)
    return pl.pallas_call(
        functools.partial(_rw_pre_kernel, seg),
        grid=(bsz, nb),
        in_specs=[
            pl.BlockSpec((SEQ_BLOCK, COLS_A), tok),
            full(1, COLS_A), full(2, 1, D_A), full(2, 128, D_A), full(2, 1, D_A), full(2, 128, D_A),
            full(128, D_A), full(1, D_A), full(1, D_A), full(1, D_A),
        ],
        out_specs=[s1, s1, s1, s2, s2, s2, s1, s1],
        out_shape=[one, one, one, two, two, two, one, one],
        compiler_params=_params(("arbitrary", "arbitrary")),
        name="rwkv_pre",
    )(p, lw["mu"], lw["w0"], lw["w2p"], lw["a0"], lw["a2p"], lw["g2"], lw["kk"], lw["ka"], lw["rk"])


_inv_dot = _dot1
_attn_dot = _dot1
_state_dot = _dot1
_dft_dot = _dot1


def _unit_tri_inverses(nmats, eye, r2, c2):
    b16 = (r2 // 16) == (c2 // 16)
    b32 = (r2 // 32) == (c2 // 32)
    pws = [jnp.where(b16, n, 0.0) for n in nmats]
    xs = [eye + p for p in pws]
    for _ in range(3):
        pws = [_inv_dot(p, p) for p in pws]
        xs = [x + _inv_dot(x, p) for x, p in zip(xs, pws)]
    for off_mask in (b32 & ~b16, ~b32):
        ts = [_inv_dot(x, jnp.where(off_mask, n, 0.0)) for x, n in zip(xs, nmats)]
        xs = [x + _inv_dot(t, x) for x, t in zip(xs, ts)]
    return xs


def _rw_scan_kernel(nb, rf_ref, vf_ref, kkf_ref, rb_ref, vb_ref, kkb_ref, kdf_ref, bbf_ref, lwf_ref,
                    kdb_ref, bbb_ref, lwb_ref, s0_ref, yf_ref, yb_ref, sfin_ref, s_scr):
    i = pl.program_id(1)

    @pl.when(i == 0)
    def _():
        s_scr[...] = s0_ref[...]

    c = CHUNK
    nseq = rf_ref.shape[0]
    nsub = rf_ref.shape[1] // c
    n_pairs = N_HEADS_A // 2
    row = _iota((c, c), 0)
    col = _iota((c, c), 1)
    r2 = _iota((2 * c, 2 * c), 0)
    c2 = _iota((2 * c, 2 * c), 1)
    hmask = (r2 // c) == (c2 // c)
    tr = r2 % c
    tc = c2 % c
    eye = (r2 == c2).astype(F32)
    lane_lo = _iota((c, 2 * c), 1) < c
    tri = [(col <= row).astype(BF16), (col >= row).astype(BF16)]
    m_strict = [hmask & (tr > tc), hmask & (tr < tc)]
    m_incl = [hmask & (tr >= tc), hmask & (tr <= tc)]
    dirs = ((rf_ref, vf_ref, kkf_ref, kdf_ref, bbf_ref, lwf_ref, yf_ref),
            (rb_ref, vb_ref, kkb_ref, kdb_ref, bbb_ref, lwb_ref, yb_ref))

    def stack_masked(x):
        return jnp.concatenate([jnp.where(lane_lo, x, 0.0), jnp.where(lane_lo, 0.0, x)], axis=0)

    def twice(x):
        return jnp.concatenate([x, x], axis=0)

    def body(j, carry):
        chains = []
        for q, (d, (r_ref, v_ref, kk_ref, kd_ref, bb_ref, lw_ref, y_ref)) in (
                (q, dd) for q in range(nseq) for dd in enumerate(dirs)):
            rev = d == 1
            sub = (nsub - 1 - j) if rev else j
            sl = pl.ds(pl.multiple_of(sub * c, c), c)
            lw = lw_ref[q, sl, :]
            cl = _dot_exact_lhs(tri[d], lw)
            tot = cl[0:1, :] if rev else cl[c - 1:c, :]
            e_out = jnp.exp(-cl)
            e_rest = jnp.exp(tot - cl)
            rt = r_ref[q, sl, :] * jnp.exp(cl)
            at = -kk_ref[q, sl, :] * jnp.exp(cl - lw)
            bb = bb_ref[q, sl, :]
            kd = kd_ref[q, sl, :]
            v = v_ref[q, sl, :]
            ptot = jnp.exp(tot)
            for pr in range(n_pairs):
                ls = slice(2 * c * pr, 2 * c * pr + 2 * c)
                chains.append(dict(
                    q=q, d=d, pr=pr, sl=sl, ls=ls, y_ref=y_ref, ptot=ptot[:, ls],
                    a_s=stack_masked(at[:, ls]), r_s=stack_masked(rt[:, ls]), v2=stack_masked(v[:, ls]),
                    bk=jnp.concatenate([twice((bb * e_out)[:, ls]), twice((kd * e_out)[:, ls])], axis=0),
                    bkp=jnp.concatenate([twice((bb * e_rest)[:, ls]), twice((kd * e_rest)[:, ls])], axis=0)))
        gs = [_attn_dot(jnp.concatenate([ch["a_s"], ch["r_s"]], axis=0), ch["bk"], NT) for ch in chains]
        tinvs = _unit_tri_inverses([jnp.where(m_strict[ch["d"]], g[0:2 * c, 0:2 * c], 0.0)
                                    for ch, g in zip(chains, gs)], eye, r2, c2)
        ss = [s_scr[ch["q"], ch["d"], ch["pr"]] for ch in chains]
        rhs = [_state_dot(ch["a_s"], s, NT)
               + _attn_dot(jnp.where(m_strict[ch["d"]], g[0:2 * c, 2 * c:4 * c], 0.0), ch["v2"])
               for ch, g, s in zip(chains, gs, ss)]
        ypart = [_state_dot(ch["r_s"], s, NT)
                 + _attn_dot(jnp.where(m_incl[ch["d"]], g[2 * c:4 * c, 2 * c:4 * c], 0.0), ch["v2"])
                 for ch, g, s in zip(chains, gs, ss)]
        us = [_attn_dot(t, x) for t, x in zip(tinvs, rhs)]
        ys = [yp + _attn_dot(jnp.where(m_incl[ch["d"]], g[2 * c:4 * c, 0:2 * c], 0.0), u)
              for ch, g, yp, u in zip(chains, gs, ypart, us)]
        upds = [_state_dot(jnp.concatenate([u, ch["v2"]], axis=0), ch["bkp"], TN) for ch, u in zip(chains, us)]
        for ch, y, s, upd in zip(chains, ys, ss, upds):
            ch["y_ref"][ch["q"], ch["sl"], ch["ls"]] = y[0:c, :] + y[c:2 * c, :]
            s_scr[ch["q"], ch["d"], ch["pr"]] = s * ch["ptot"] + jnp.where(hmask, upd, 0.0)
        return carry

    lax.fori_loop(0, nsub, body, 0)

    @pl.when(i == nb - 1)
    def _():
        sfin_ref[...] = s_scr[...]


def _rw_scan(pre, s0_pairs, bsz, nb):
    r, v, kk, kd, bb, lw = pre
    n = bsz * nb * SEQ_BLOCK
    ns = SCAN_SEQS
    assert bsz % ns == 0
    seq3 = lambda t: t.reshape(bsz, nb * SEQ_BLOCK, D_A)
    seq4 = lambda t: t.reshape(2, bsz, nb * SEQ_BLOCK, D_A)
    fwd = pl.BlockSpec((ns, SEQ_BLOCK, D_A), lambda b, i: (b, i, 0))
    bwd = pl.BlockSpec((ns, SEQ_BLOCK, D_A), lambda b, i: (b, nb - 1 - i, 0))
    fwd2 = pl.BlockSpec((None, ns, SEQ_BLOCK, D_A), lambda b, i: (0, b, i, 0))
    bwd2 = pl.BlockSpec((None, ns, SEQ_BLOCK, D_A), lambda b, i: (1, b, nb - 1 - i, 0))
    sst = pl.BlockSpec((ns, 2, 3, 2 * HEAD, 2 * HEAD), lambda b, i: (b, 0, 0, 0, 0))
    y_shape = jax.ShapeDtypeStruct((bsz, nb * SEQ_BLOCK, D_A), F32)
    r, v, kk = seq3(r), seq3(v), seq3(kk)
    kd, bb, lw = seq4(kd), seq4(bb), seq4(lw)
    yf, yb, sfin = pl.pallas_call(
        functools.partial(_rw_scan_kernel, nb),
        grid=(bsz // ns, nb),
        in_specs=[fwd, fwd, fwd, bwd, bwd, bwd, fwd2, fwd2, fwd2, bwd2, bwd2, bwd2, sst],
        out_specs=[fwd, bwd, sst],
        out_shape=[y_shape, y_shape, jax.ShapeDtypeStruct((bsz, 2, 3, 2 * HEAD, 2 * HEAD), F32)],
        scratch_shapes=[pltpu.VMEM((ns, 2, 3, 2 * HEAD, 2 * HEAD), F32)],
        compiler_params=_params(("arbitrary", "arbitrary")),
        name="rwkv_scan",
    )(r, v, kk, r, v, kk, kd, bb, lw, kd, bb, lw, s0_pairs)
    return yf.reshape(n, D_A), yb.reshape(n, D_A), sfin


def _to_pairs(s):
    lead = s.shape[:-3]
    s = s.reshape(lead + (3, 2, HEAD, HEAD))
    z = jnp.zeros_like(s[..., 0, :, :])
    top = jnp.concatenate([s[..., 0, :, :], z], axis=-1)
    bot = jnp.concatenate([z, s[..., 1, :, :]], axis=-1)
    return jnp.concatenate([top, bot], axis=-2)


def _from_pairs(sp):
    lead = sp.shape[:-3]
    return jnp.stack([sp[..., :HEAD, :HEAD], sp[..., HEAD:, HEAD:]], axis=-3).reshape(lead + (6, HEAD, HEAD))


def _lru_kernel(rev, nb, seg, p_ref, cw_ref, cb_ref, wg_ref, bg_ref, lam_ref, h0_ref, h_ref, hfin_ref, hc_scr):
    i = pl.program_id(1)

    @pl.when(i == 0)
    def _():
        hc_scr[...] = h0_ref[0]

    u_in = p_ref[...]
    n = u_in.shape[0]
    t = _iota((n, 1), 0)
    pos = t % seg
    cw = cw_ref[...]
    xm = (cw[0:1] * _shifted(u_in, -2, pos, seg) + cw[1:2] * _shifted(u_in, -1, pos, seg)
          + cw[2:3] * u_in + cw[3:4] * _shifted(u_in, 1, pos, seg)) + cb_ref[...]
    g = _dot3(xm, wg_ref[...]) + bg_ref[...]
    gr = _sigmoid(g[:, 0:D_C])
    gi = _sigmoid(g[:, D_C:2 * D_C])
    lam = lam_ref[...]
    softplus_neg_lam = jnp.maximum(-lam, 0.0) + jnp.log1p(jnp.exp(-jnp.abs(lam)))
    log_a = -LRU_C * gr * softplus_neg_lam
    a = jnp.exp(log_a)
    u = jnp.sqrt(-jnp.tanh(log_a) * (a * a + 1.0)) * gi * xm
    s = 1
    while s < n:
        if rev:
            ok = t < n - s
            a_sh = jnp.where(ok, pltpu.roll(a, n - s, 0), 1.0)
            u_sh = jnp.where(ok, pltpu.roll(u, n - s, 0), 0.0)
        else:
            ok = t >= s
            a_sh = jnp.where(ok, pltpu.roll(a, s, 0), 1.0)
            u_sh = jnp.where(ok, pltpu.roll(u, s, 0), 0.0)
        u = a * u_sh + u
        a = a * a_sh
        s *= 2
    h = a * hc_scr[...] + u
    h_ref[...] = h
    hc_scr[...] = h[0:1, :] if rev else h[n - 1:n, :]

    @pl.when(i == nb - 1)
    def _():
        hfin_ref[0] = hc_scr[...]


def _lru(p, lw, h0, d, off, bsz, nb, seg):
    rev = d == 1
    n = bsz * nb * SEQ_BLOCK
    blk = (lambda i: nb - 1 - i) if rev else (lambda i: i)
    pcol = (COLS_A + COLS_B) // D_C
    full = lambda *shape: pl.BlockSpec(shape, lambda b, i: (0,) * len(shape))
    dsel = lambda *shape: pl.BlockSpec((None,) + shape, lambda b, i: (d,) + (0,) * len(shape))
    return pl.pallas_call(
        functools.partial(_lru_kernel, rev, nb, seg),
        grid=(bsz, nb),
        in_specs=[
            pl.BlockSpec((SEQ_BLOCK, D_C), lambda b, i: (off + b * nb + blk(i), pcol)),
            full(4, D_C), full(1, D_C), dsel(D_C, 2 * D_C), dsel(1, 2 * D_C), dsel(1, D_C),
            pl.BlockSpec((1, None, 1, D_C), lambda b, i: (b, d, 0, 0)),
        ],
        out_specs=[pl.BlockSpec((SEQ_BLOCK, D_C), lambda b, i: (b * nb + blk(i), 0)),
                   pl.BlockSpec((1, 1, D_C), lambda b, i: (b, 0, 0))],
        out_shape=[jax.ShapeDtypeStruct((n, D_C), F32), jax.ShapeDtypeStruct((bsz, 1, D_C), F32)],
        scratch_shapes=[pltpu.VMEM((1, D_C), F32)],
        compiler_params=_params(("arbitrary", "arbitrary")),
        name="rglru_bwd" if rev else "rglru_fwd",
    )(p, lw["cw"], lw["cb"], lw["wg"], lw["bg"], lw["lam"], h0)


def _rw_post_kernel(yf_ref, yb_ref, bonus_ref, gate_ref, g_ref, b_ref, o_ref):
    y = yf_ref[...] + yb_ref[...]
    ones = _head_ones(D_A)
    mean = _dot_exact_rhs(y, ones) * (1.0 / HEAD)
    yc = y - mean
    var = _dot_exact_rhs(yc * yc, ones) * (1.0 / HEAD)
    yn = yc * lax.rsqrt(var + LNX_EPS) * g_ref[...] + b_ref[...]
    o_ref[...] = (yn + bonus_ref[...]) * gate_ref[...]


def _rw_post(yf, yb, bonus, gate, lw):
    n = yf.shape[0]
    s1 = pl.BlockSpec((TOK_TILE, D_A), lambda i: (i, 0))
    w1 = pl.BlockSpec((1, D_A), lambda i: (0, 0))
    return pl.pallas_call(
        _rw_post_kernel,
        grid=(n // TOK_TILE,),
        in_specs=[s1, s1, s1, s1, w1, w1],
        out_specs=s1,
        out_shape=jax.ShapeDtypeStruct((n, D_A), F32),
        compiler_params=_params(("arbitrary",)),
        name="rwkv_post",
    )(yf, yb, bonus, gate, lw["lnx_g"], lw["lnx_b"])


def _rw_weights(lp):
    def pad_dir(w, slot):
        z = jnp.zeros((2, 128, D_A), F32)
        z = z.at[0, 0:64].set(w[0])
        return z.at[1, 64:128].set(w[1])
    return {
        "mu": lp["rw_mu"].reshape(1, COLS_A),
        "w0": lp["rw_w0"].reshape(2, 1, D_A), "w2p": pad_dir(lp["rw_w2"], 0),
        "a0": lp["rw_a0"].reshape(2, 1, D_A), "a2p": pad_dir(lp["rw_a2"], 1),
        "g2": lp["rw_g2"], "kk": lp["rw_kk"].reshape(1, D_A), "ka": lp["rw_ka"].reshape(1, D_A),
        "rk": lp["rw_rk"].reshape(1, D_A),
        "lnx_g": lp["rw_lnx_g"].reshape(1, D_A), "lnx_b": lp["rw_lnx_b"].reshape(1, D_A),
    }


def _rwkv_mixer(p, lw, s0, off, bsz, nb, seg):
    r, v, kk, kd, bb, lwd, bonus, gate = _rw_pre(p, lw, off, bsz, nb, seg)
    yf, yb, sfin = _rw_scan((r, v, kk, kd, bb, lwd), _to_pairs(s0), bsz, nb)
    return _rw_post(yf, yb, bonus, gate, lw), _from_pairs(sfin)


def _rwkv_parts(p, lw, s0, off, bsz, nb, seg):
    r, v, kk, kd, bb, lwd, bonus, gate = _rw_pre(p, lw, off, bsz, nb, seg)
    yf, yb, sfin = _rw_scan((r, v, kk, kd, bb, lwd), _to_pairs(s0), bsz, nb)
    return (yf, yb, bonus, gate), _from_pairs(sfin)


def _lru_weights(lp):
    def block_diag(w):
        eye = jnp.eye(N_HEADS_A, dtype=F32)
        return jnp.einsum("hij,hg->higj", w, eye).reshape(D_C, D_C)
    wg = jnp.stack([jnp.concatenate([block_diag(lp["lru_wa"][d]), block_diag(lp["lru_wx"][d])], axis=1)
                    for d in range(2)])
    bg = jnp.stack([jnp.concatenate([lp["lru_ba"][d], lp["lru_bx"][d]])[None] for d in range(2)])
    return {"cw": lp["lru_conv_w"], "cb": lp["lru_conv_b"].reshape(1, D_C), "wg": wg, "bg": bg,
            "lam": lp["lru_lam"].reshape(2, 1, D_C)}


def _lru_mixer(p, lw, h0, off, bsz, nb, seg):
    h0 = h0.reshape(bsz, 2, 1, D_C)
    hf, ff = _lru(p, lw, h0, 0, off, bsz, nb, seg)
    hb, fb = _lru(p, lw, h0, 1, off, bsz, nb, seg)
    return hf, hb, jnp.concatenate([ff, fb], axis=1)


def _dft_tables(seq):
    n = 2 * seq
    n2 = 128 if n >= 4096 else 16
    n1 = n // n2
    n1h = n1 // 2
    i1 = np.arange(n1h)[None, None, :]
    k1 = np.arange(n1)[None, :, None]
    i2 = np.arange(n2)[:, None, None]
    ph = 2.0 * np.pi * (i1 * k1 / n1 + i2 * k1 / n)
    f1 = np.concatenate([np.cos(ph), -np.sin(ph)], axis=1)
    g3 = np.concatenate([np.cos(ph), -np.sin(ph)], axis=1).transpose(0, 2, 1) / n
    a = 2.0 * np.pi * np.outer(np.arange(n2), np.arange(n2)) / n2
    f2r, f2i = np.cos(a), -np.sin(a)
    f2 = np.block([[f2r, -f2i], [f2i, f2r]])
    f2c = np.block([[f2r, f2i], [-f2i, f2r]])
    as32 = lambda t: jnp.asarray(t, F32)
    return n1, n2, as32(f1), as32(f2), as32(f2c), as32(g3)


def _dft_stage1(src_ref, dst_ref, f1_ref, n1, n2):
    n1h = n1 // 2

    def body(i, carry):
        xs = src_ref[pl.ds(i, n1h, stride=n2), :]
        a = _dft_dot(f1_ref[i], xs)
        dst_ref[pl.ds(i, n1, stride=2 * n2), :] = a[0:n1]
        dst_ref[pl.ds(n2 + i, n1, stride=2 * n2), :] = a[n1:2 * n1]
        return carry

    lax.fori_loop(0, n2, body, 0, unroll=DFT_UNROLL)


def _hy_filter_kernel(n1, n2, z_ref, w1_ref, b1_ref, w2_ref, b2_ref, w3f_ref, w3b_ref, b3f_ref, b3b_ref,
                      dcf_ref, dcb_ref, freq_ref, f1_ref, f2_ref, o_ref, hf_scr, hb_scr, xf_scr, xb_scr):
    z = z_ref[...]
    h = jnp.sin(freq_ref[0:1, :] * (_dot3(z, w1_ref[...]) + b1_ref[...]))
    h = jnp.sin(freq_ref[1:2, :] * (_dot3(h, w2_ref[...]) + b2_ref[...]))
    t = z[:, 0:1]
    hf_scr[...] = (_dot3(h, w3f_ref[...]) + b3f_ref[...]) * jnp.exp(-t * jnp.abs(dcf_ref[...]))
    hb = (_dot3(h, w3b_ref[...]) + b3b_ref[...]) * jnp.exp(-t * jnp.abs(dcb_ref[...]))
    hb_scr[...] = jnp.where(_iota(hb.shape, 0) == 0, 0.0, hb)
    _dft_stage1(hf_scr, xf_scr, f1_ref, n1, n2)
    _dft_stage1(hb_scr, xb_scr, f1_ref, n1, n2)
    f2 = f2_ref[...]

    def body(k, carry):
        sl = pl.ds(pl.multiple_of(k * 2 * n2, 2 * n2), 2 * n2)
        xf = _dot3(f2, xf_scr[sl, :])
        xb = _dot3(f2, xb_scr[sl, :])
        o_ref[sl, :] = jnp.concatenate([xf[0:n2] + xb[0:n2], xf[n2:2 * n2] - xb[n2:2 * n2]], axis=0)
        return carry

    lax.fori_loop(0, n1, body, 0)


def _hy_filter(lw, seq, tables):
    n1, n2, f1, f2, _, _ = tables
    ct = LANES
    nct = D_B // ct
    full = lambda a: pl.BlockSpec(a.shape, lambda j: (0,) * a.ndim)
    col = lambda rows, shift: pl.BlockSpec((rows, ct), lambda j: (0, shift + j))
    pos = jnp.arange(seq, dtype=F32)
    t = pos[:, None] / (seq - 1)
    bands = jnp.linspace(1e-4, N_BANDS - 1, N_BANDS, dtype=F32)
    ang = (2.0 * math.pi / seq) * pos[:, None] * bands[None, :]
    z = jnp.concatenate([t, jnp.cos(ang), -jnp.sin(ang), jnp.zeros((seq, 64 - 1 - 2 * N_BANDS), F32)], axis=-1)
    args = [z, lw["w1p"], lw["b1"], lw["w2"], lw["b2"]]
    return pl.pallas_call(
        functools.partial(_hy_filter_kernel, n1, n2),
        grid=(nct,),
        in_specs=[full(a) for a in args] + [
            col(64, 0), col(64, nct), col(1, 0), col(1, nct), col(1, 0), col(1, nct),
            full(lw["freq"]), full(f1), full(f2)],
        out_specs=pl.BlockSpec((4 * seq, ct), lambda j: (0, j)),
        out_shape=jax.ShapeDtypeStruct((4 * seq, D_B), F32),
        scratch_shapes=[pltpu.VMEM((seq, ct), F32), pltpu.VMEM((seq, ct), F32),
                        pltpu.VMEM((4 * seq, ct), F32), pltpu.VMEM((4 * seq, ct), F32)],
        compiler_params=_params(("arbitrary",)),
        name="hyena_filter",
    )(*args, lw["w3"], lw["w3"], lw["b3"], lw["b3"], lw["decay"], lw["decay"], lw["freq"], f1, f2)


def _hyena_kernel(n1, n2, seg, pv_ref, px1_ref, px2_ref, cwv_ref, cwx1_ref, cwx2_ref, cbv_ref, cbx1_ref, cbx2_ref,
                  bias_ref, h_ref, f1_ref, f2_ref, f2c_ref, g3_ref, o_ref, z_scr, x_scr, y_scr):
    seq = pv_ref.shape[0]
    pos = _iota((seq, 1), 0) % seg

    def conv(p_ref, w_ref, b_ref):
        p = p_ref[...]
        w = w_ref[...]
        return w[0:1] * _shifted(p, -1, pos, seg) + w[1:2] * p + w[2:3] * _shifted(p, 1, pos, seg) + b_ref[...]

    z_scr[...] = conv(pv_ref, cwv_ref, cbv_ref) * conv(px1_ref, cwx1_ref, cbx1_ref)
    _dft_stage1(z_scr, x_scr, f1_ref, n1, n2)
    f2 = f2_ref[...]
    f2c = f2c_ref[...]

    def mid(k, carry):
        sl = pl.ds(pl.multiple_of(k * 2 * n2, 2 * n2), 2 * n2)
        x = _dft_dot(f2, x_scr[sl, :])
        hh = h_ref[sl, :]
        xr, xi = x[0:n2], x[n2:2 * n2]
        hr, hi = hh[0:n2], hh[n2:2 * n2]
        y = jnp.concatenate([xr * hr - xi * hi, xr * hi + xi * hr], axis=0)
        x_scr[sl, :] = _dft_dot(f2c, y)
        return carry

    lax.fori_loop(0, n1, mid, 0, unroll=DFT_UNROLL)
    n1h = n1 // 2

    def last(i, carry):
        d = jnp.concatenate([x_scr[pl.ds(i, n1, stride=2 * n2), :], x_scr[pl.ds(n2 + i, n1, stride=2 * n2), :]], axis=0)
        y_scr[pl.ds(i, n1h, stride=n2), :] = _dft_dot(g3_ref[i], d)
        return carry

    lax.fori_loop(0, n2, last, 0, unroll=DFT_UNROLL)
    o_ref[...] = conv(px2_ref, cwx2_ref, cbx2_ref) * (y_scr[...] + z_scr[...] * bias_ref[...])


def _hyena(p, lw, hspec, tables, off_seq, bsz, seq, seg):
    n1, n2, f1, f2, f2c, g3 = tables
    ct = LANES
    nct = D_B // ct
    c0 = COLS_A // ct
    pcol = lambda g: pl.BlockSpec((seq, ct), lambda j, b: (off_seq + b, c0 + g * nct + j))
    wcol = lambda rows, g: pl.BlockSpec((rows, ct), lambda j, b: (0, g * nct + j))
    once = lambda a: pl.BlockSpec(a.shape, lambda j, b: (0,) * a.ndim)
    return pl.pallas_call(
        functools.partial(_hyena_kernel, n1, n2, seg),
        grid=(nct, bsz),
        in_specs=[pcol(0), pcol(1), pcol(2), wcol(3, 0), wcol(3, 1), wcol(3, 2), wcol(1, 0), wcol(1, 1), wcol(1, 2),
                  wcol(1, 0), pl.BlockSpec((4 * seq, ct), lambda j, b: (0, j)),
                  once(f1), once(f2), once(f2c), once(g3)],
        out_specs=pl.BlockSpec((seq, ct), lambda j, b: (b, j)),
        out_shape=jax.ShapeDtypeStruct((bsz * seq, D_B), F32),
        scratch_shapes=[pltpu.VMEM((seq, ct), F32), pltpu.VMEM((4 * seq, ct), F32), pltpu.VMEM((seq, ct), F32)],
        compiler_params=_params(("arbitrary", "arbitrary")),
        name="hyena",
    )(p, p, p, lw["cw"], lw["cw"], lw["cw"], lw["cb"], lw["cb"], lw["cb"], lw["bias"], hspec, f1, f2, f2c, g3)


def _hy_weights(lp):
    pos_dim = 1 + 2 * N_BANDS
    return {
        "w1p": jnp.zeros((64, 64), F32).at[0:pos_dim].set(lp["hy_f_w1"]), "b1": lp["hy_f_b1"].reshape(1, 64),
        "w2": lp["hy_f_w2"], "b2": lp["hy_f_b2"].reshape(1, 64),
        "w3": lp["hy_f_w3"], "b3": lp["hy_f_b3"].reshape(1, 2 * D_B), "decay": lp["hy_decay"].reshape(1, 2 * D_B),
        "freq": lp["hy_freq"], "cw": lp["hy_conv_w"], "cb": lp["hy_conv_b"].reshape(1, COLS_B),
        "bias": lp["hy_bias"].reshape(1, D_B),
    }


def _hyena_mixer(p, lw, off_seq, bsz, seq, seg):
    tables = _dft_tables(seq)
    hspec = _hy_filter(lw, seq, tables)
    return _hyena(p, lw, hspec, tables, off_seq, bsz, seq, seg)


def _gelu_tanh(x):
    return 0.5 * x * (1.0 + jnp.tanh(math.sqrt(2.0 / math.pi) * (x + 0.044715 * (x * x * x))))


def _outproj_kernel(yf_ref, yr_ref, bonus_ref, gate_ref, lng_ref, lnb_ref, yb_ref, hf_ref, hb_ref, yg_ref, x_ref,
                    mod_ref, w_ref, gb_ref, gc_ref, g2_ref, rw_ref, rb_ref, x1_ref, h2_ref, lg_ref):
    ysum = yf_ref[...] + yr_ref[...]
    ones = _head_ones(D_A)
    yc = ysum - _dot_exact_rhs(ysum, ones) * (1.0 / HEAD)
    var = _dot_exact_rhs(yc * yc, ones) * (1.0 / HEAD)
    ya = (yc * lax.rsqrt(var + LNX_EPS) * lng_ref[...] + lnb_ref[...] + bonus_ref[...]) * gate_ref[...]
    ybn = _rms(yb_ref[...]) * gb_ref[...]
    ycn = _rms((hf_ref[...] + hb_ref[...]) * _gelu_tanh(yg_ref[...])) * gc_ref[...]
    y = (_dot1(ya, w_ref[0:D_A, :]) + _dot1(ybn, w_ref[D_A:D_A + D_B, :])
         + _dot1(ycn, w_ref[D_A + D_B:D_A + D_B + D_C, :]))
    x1 = x_ref[...] + mod_ref[0, 2:3, :] * y
    x1_ref[...] = x1
    h2 = _rms(x1) * g2_ref[...] * (1.0 + mod_ref[0, 4:5, :]) + mod_ref[0, 3:4, :]
    _to_row_tiles(h2_ref, h2)
    lg_ref[...] = _dot3(rw_ref[...], h2, NT) + rb_ref[...]


def _out_proj(rw_parts, yb, hf, hb, p, x, mod, lp, mod_row):
    n_tok, d = x.shape
    tile = lambda w: pl.BlockSpec((TOK_TILE, w), lambda i: (i, 0))
    full = lambda *shape: pl.BlockSpec(shape, lambda i: (0,) * len(shape))
    yg_col = (COLS_A + COLS_B + D_C) // D_C
    return pl.pallas_call(
        _outproj_kernel,
        grid=(n_tok // TOK_TILE,),
        in_specs=[tile(D_A), tile(D_A), tile(D_A), tile(D_A), full(1, D_A), full(1, D_A),
                  tile(D_B), tile(D_C), tile(D_C),
                  pl.BlockSpec((TOK_TILE, D_C), lambda i: (i, yg_col)), tile(d),
                  pl.BlockSpec((1, 6, d), lambda i: (mod_row(i), 0, 0)),
                  full(d, d), full(1, D_B), full(1, D_C), full(1, d), full(N_EXPERTS, d), full(N_EXPERTS, 1)],
        out_specs=[tile(d), pl.BlockSpec((TOK_TILE * ROW_TILE, LANES), lambda i: (i, 0)),
                   pl.BlockSpec((N_EXPERTS, TOK_TILE), lambda i: (0, i))],
        out_shape=[jax.ShapeDtypeStruct((n_tok, d), F32), jax.ShapeDtypeStruct((n_tok * ROW_TILE, LANES), F32),
                   jax.ShapeDtypeStruct((N_EXPERTS, n_tok), F32)],
        compiler_params=_params(("arbitrary",)),
        name="out_proj",
    )(*rw_parts, lp["rw_lnx_g"].reshape(1, D_A), lp["rw_lnx_b"].reshape(1, D_A),
      yb, hf, hb, p, x, mod, lp["w_out"].astype(BF16), lp["hy_out_g"].reshape(1, D_B),
      lp["lru_out_g"].reshape(1, D_C), lp["norm2_g"].reshape(1, d), lp["router_w"].T,
      lp["router_b"].reshape(N_EXPERTS, 1))


def _route_kernel(lg_ref, eidx_ref, gate_ref, rank_ref, cnt_ref, run_scr):
    i = pl.program_id(0)

    @pl.when(i == 0)
    def _():
        run_scr[...] = jnp.zeros_like(run_scr)

    l = lg_ref[...]
    ne, t = l.shape
    rowi = _iota((ne, t), 0)
    row8 = _iota((8, t), 0)
    tops, hots = [], []
    oh_all = jnp.zeros((ne, t), F32)
    eidx = jnp.zeros((8, t), jnp.int32)
    for j in range(TOP_K):
        m = jnp.max(l, axis=0, keepdims=True)
        idx = jnp.min(jnp.where(l == m, rowi, ne), axis=0, keepdims=True)
        oh = rowi == idx
        tops.append(m)
        hots.append(oh)
        oh_all = oh_all + oh.astype(F32)
        eidx = jnp.where(row8 == j, idx, eidx)
        l = jnp.where(oh, -jnp.inf, l)
    es = [jnp.exp(m - tops[0]) for m in tops]
    denom = es[0] + es[1] + es[2] + es[3]
    before = (_iota((t, t), 0) < _iota((t, t), 1)).astype(BF16)
    cum = _dg(oh_all.astype(BF16), before) + run_scr[...]
    gates = jnp.zeros((8, t), F32)
    rank = jnp.zeros((8, t), jnp.int32)
    for j in range(TOP_K):
        gates = jnp.where(row8 == j, es[j] / denom, gates)
        rj = jnp.sum(jnp.where(hots[j], cum, 0.0), axis=0, keepdims=True)
        rank = jnp.where(row8 == j, rj.astype(jnp.int32), rank)
    eidx_ref[...] = eidx
    gate_ref[...] = gates
    rank_ref[...] = rank
    run = run_scr[...] + jnp.sum(oh_all, axis=1, keepdims=True)
    run_scr[...] = run
    cnt_ref[...] = jnp.broadcast_to(run, cnt_ref.shape)


def _route(logits_t):
    ne, n_tok = logits_t.shape
    t = ROUTE_TILE
    row = lambda dt: jax.ShapeDtypeStruct((8, n_tok), dt)
    spec = pl.BlockSpec((8, t), lambda i: (0, i))
    return pl.pallas_call(
        _route_kernel,
        grid=(n_tok // t,),
        in_specs=[pl.BlockSpec((ne, t), lambda i: (0, i))],
        out_specs=[spec, spec, spec, pl.BlockSpec((ne, LANES), lambda i: (0, 0))],
        out_shape=[row(jnp.int32), row(F32), row(jnp.int32), jax.ShapeDtypeStruct((ne, LANES), F32)],
        scratch_shapes=[pltpu.VMEM((ne, 1), F32)],
        compiler_params=_params(("arbitrary",)),
        name="moe_route",
    )(logits_t)


ROW_TILE = 8


def _to_row_tiles(ref, x):
    n = x.shape[0]
    for s in range(ROW_TILE):
        ref[pl.ds(s, n, stride=ROW_TILE), :] = x[:, s * LANES:(s + 1) * LANES]


def _from_row_tiles(ref, start, n):
    return jnp.concatenate([ref[pl.ds(start * ROW_TILE + s, n, stride=ROW_TILE), :] for s in range(ROW_TILE)],
                           axis=1)


def _row_tile(ref, r):
    return ref.at[pl.ds(pl.multiple_of(r * ROW_TILE, ROW_TILE), ROW_TILE), :]


def _gather_rows(idx_smem, src_hbm, dst, sem, n_rows):
    def body(r, carry):
        pltpu.make_async_copy(_row_tile(src_hbm, idx_smem[r]), _row_tile(dst, r), sem).start()
        return carry
    lax.fori_loop(0, n_rows, body, 0, unroll=GATHER_UNROLL)


def _wait_rows(src_hbm, dst, sem, n_rows):
    def body(r, carry):
        pltpu.make_async_copy(_row_tile(src_hbm, 0), _row_tile(dst, r), sem).wait()
        return carry
    lax.fori_loop(0, n_rows, body, 0, unroll=GATHER_UNROLL)


def _prefetched_gather(i, n_steps, idx_cur, idx_nxt, src_hbm, idx_smem, buf, idx_sem, row_sem, n_rows):
    def issue(idx_ref, slot):
        cp = pltpu.make_async_copy(idx_ref.at[0, 0], idx_smem, idx_sem)
        cp.start()
        cp.wait()
        _gather_rows(idx_smem, src_hbm, buf.at[slot], row_sem.at[slot], n_rows)

    @pl.when(i == 0)
    def _():
        issue(idx_cur, 0)

    @pl.when(i + 1 < n_steps)
    def _():
        issue(idx_nxt, (i + 1) % 2)

    _wait_rows(src_hbm, buf.at[i % 2], row_sem.at[i % 2], n_rows)


def _idx_specs(n_steps, n_rows, index_args):
    cur = pl.BlockSpec((1, 1, n_rows), lambda i, *_: (i, 0, 0))
    nxt = pl.BlockSpec((1, 1, n_rows), lambda i, *_: (jnp.minimum(i + 1, n_steps - 1), 0, 0))
    return [cur, nxt]


DISPATCH_TILE = 256


def _dispatch_kernel(n_fill, fill_steps, idx_ref, h_ref, xs_hbm, idx_smem, zero_tile, idx_sem, row_sem):
    i = pl.program_id(0)
    tc = DISPATCH_TILE
    n_rows = TOP_K * tc

    @pl.when(i == 0)
    def _():
        zero_tile[...] = jnp.zeros_like(zero_tile)

    cp = pltpu.make_async_copy(idx_ref.at[0, 0], idx_smem, idx_sem)
    cp.start()
    cp.wait()

    for j in range(TOP_K):
        def start(t, carry):
            pltpu.make_async_copy(_row_tile(h_ref, t), _row_tile(xs_hbm, idx_smem[j * tc + t]), row_sem).start()
            return carry

        lax.fori_loop(0, tc, start, 0, unroll=GATHER_UNROLL)

    def fill_copy(t):
        return pltpu.make_async_copy(zero_tile, _row_tile(xs_hbm, idx_smem[n_rows + t]), row_sem)

    @pl.when(i < fill_steps)
    def _():
        def fill(t, carry):
            fill_copy(t).start()
            return carry

        lax.fori_loop(0, n_fill, fill, 0, unroll=GATHER_UNROLL)

    def wait_copy(r, carry):
        pltpu.make_async_copy(_row_tile(h_ref, 0), _row_tile(xs_hbm, 0), row_sem).wait()
        return carry

    lax.fori_loop(0, n_rows, wait_copy, 0, unroll=GATHER_UNROLL)

    @pl.when(i < fill_steps)
    def _():
        def wait_fill(t, carry):
            fill_copy(0).wait()
            return carry

        lax.fori_loop(0, n_fill, wait_fill, 0, unroll=GATHER_UNROLL)


def _dispatch(h2, dest_t, pad_rows, n_rows_total):
    n_steps = dest_t.shape[0]
    tc = DISPATCH_TILE
    n_pad = pad_rows.shape[0]
    n_fill = GATHER_UNROLL
    while n_fill * n_steps < n_pad:
        n_fill *= 2
    assert n_pad % n_fill == 0
    fill_steps = n_pad // n_fill
    fill_idx = jnp.concatenate([pad_rows.astype(jnp.int32).reshape(fill_steps, n_fill),
                                jnp.zeros((n_steps - fill_steps, n_fill), jnp.int32)], axis=0)
    idx = jnp.concatenate([dest_t, fill_idx], axis=1).reshape(n_steps, 1, TOP_K * tc + n_fill)
    return pl.pallas_call(
        functools.partial(_dispatch_kernel, n_fill, fill_steps),
        grid=(n_steps,),
        in_specs=[pl.BlockSpec((1, 1, TOP_K * tc + n_fill), lambda i: (i, 0, 0)),
                  pl.BlockSpec((tc * ROW_TILE, LANES), lambda i: (i, 0))],
        out_specs=pl.BlockSpec(memory_space=pl.ANY),
        out_shape=jax.ShapeDtypeStruct((n_rows_total * ROW_TILE, LANES), F32),
        scratch_shapes=[pltpu.SMEM((TOP_K * tc + n_fill,), jnp.int32), pltpu.VMEM((ROW_TILE, LANES), F32),
                        pltpu.SemaphoreType.DMA(()), pltpu.SemaphoreType.DMA(())],
        compiler_params=_params(("arbitrary",)),
        name="moe_dispatch",
    )(idx, h2)


def _expert_kernel(be_ref, nv_ref, x_ref, wg_ref, bg_ref, wu_ref, bu_ref, wd_ref, bd_ref, o_ref):
    i = pl.program_id(0)
    n_valid = nv_ref[i]

    @pl.when(n_valid > 0)
    def _():
        x = _from_row_tiles(x_ref, 0, EXPERT_ROWS).astype(BF16)
        gt = jnp.minimum(_dg(x, wg_ref[...]) + bg_ref[...], SWIGLU_LIMIT)
        up = jnp.clip(_dg(x, wu_ref[...]) + bu_ref[...], -SWIGLU_LIMIT, SWIGLU_LIMIT)
        act = (up + 1.0) * gt * _sigmoid(SWIGLU_ALPHA * gt)
        _to_row_tiles(o_ref, _dg(act.astype(BF16), wd_ref[...]) + bd_ref[...])

    @pl.when(n_valid == 0)
    def _():
        o_ref[...] = jnp.zeros_like(o_ref)


def _experts(xs, block_e, block_valid, layer, wg, bg, wu, bu, wd, bd):
    n_blocks = block_e.shape[0]
    d = ROW_TILE * LANES
    ff = wg.shape[-1]
    wspec = lambda a, b: pl.BlockSpec((None, None, a, b), lambda i, be, nv: (layer, be[i], 0, 0))
    rows = pl.BlockSpec((EXPERT_ROWS * ROW_TILE, LANES), lambda i, be, nv: (i, 0))
    return pl.pallas_call(
        _expert_kernel,
        grid_spec=pltpu.PrefetchScalarGridSpec(
            num_scalar_prefetch=2,
            grid=(n_blocks,),
            in_specs=[rows, wspec(d, ff), wspec(1, ff), wspec(d, ff), wspec(1, ff), wspec(ff, d), wspec(1, d)],
            out_specs=rows,
        ),
        out_shape=jax.ShapeDtypeStruct((n_blocks * EXPERT_ROWS * ROW_TILE, LANES), F32),
        compiler_params=_params(("arbitrary",)),
        name="moe_experts",
    )(block_e, block_valid, xs, wg, bg, wu, bu, wd, bd)


COMBINE_TILE = 128


def _combine_kernel(n_steps, final, idx_cur, idx_nxt, ys_hbm, gate_ref, x1_ref, mod_ref, fg_ref, o_ref,
                    idx_smem, buf, idx_sem, row_sem):
    i = pl.program_id(0)
    tc = COMBINE_TILE
    _prefetched_gather(i, n_steps, idx_cur, idx_nxt, ys_hbm, idx_smem, buf, idx_sem, row_sem, TOP_K * tc)
    rows = buf.at[i % 2]
    g = gate_ref[...]
    moe = g[:, 0:1] * _from_row_tiles(rows, 0, tc)
    for j in range(1, TOP_K):
        moe = moe + g[:, j:j + 1] * _from_row_tiles(rows, j * tc, tc)
    x2 = x1_ref[...] + mod_ref[0, 5:6, :] * moe
    o_ref[...] = _rms(x2) * fg_ref[...] if final else x2


def _combine(dest_t, ys, gates_t, x1, mod, final_g, mod_row, final):
    n_tok, d = x1.shape
    tc = COMBINE_TILE
    n_steps = n_tok // tc
    per = TOK_TILE // tc
    return pl.pallas_call(
        functools.partial(_combine_kernel, n_steps, final),
        grid=(n_steps,),
        in_specs=_idx_specs(n_steps, TOP_K * tc, 1) + [
                  pl.BlockSpec(memory_space=pl.ANY),
                  pl.BlockSpec((tc, TOP_K), lambda i: (i, 0)),
                  pl.BlockSpec((tc, d), lambda i: (i, 0)),
                  pl.BlockSpec((1, 6, d), lambda i: (mod_row(i // per), 0, 0)),
                  pl.BlockSpec((1, d), lambda i: (0, 0))],
        out_specs=pl.BlockSpec((tc, d), lambda i: (i, 0)),
        out_shape=jax.ShapeDtypeStruct((n_tok, d), F32),
        scratch_shapes=[pltpu.SMEM((TOP_K * tc,), jnp.int32), pltpu.VMEM((2, TOP_K * tc * ROW_TILE, LANES), F32),
                        pltpu.SemaphoreType.DMA(()), pltpu.SemaphoreType.DMA((2,))],
        compiler_params=_params(("arbitrary",)),
        name="moe_combine",
    )(dest_t.reshape(n_steps, 1, TOP_K * tc), dest_t.reshape(n_steps, 1, TOP_K * tc), ys, gates_t, x1, mod,
      final_g.reshape(1, d))


def _moe(h2, logits_t, x1, mod, mod_row, layer, ew, final_g, final):
    n_tok = x1.shape[0]
    eidx, gates, rank, cnt = _route(logits_t)
    counts = cnt[:, 0].astype(jnp.int32)
    blk = EXPERT_ROWS
    padded = (counts + blk - 1) // blk * blk
    pad_end = jnp.cumsum(padded)
    pad_start = pad_end - padded
    e4 = eidx[0:TOP_K]
    start_of = jnp.zeros_like(e4)
    for e in range(N_EXPERTS):
        start_of = start_of + jnp.where(e4 == e, pad_start[e], 0)
    dest = start_of + rank[0:TOP_K]
    n_blocks = n_tok * TOP_K // blk + N_EXPERTS
    block_row0 = jnp.arange(n_blocks, dtype=jnp.int32) * blk
    block_e = jnp.minimum(jnp.sum((pad_end[None, :] <= block_row0[:, None]).astype(jnp.int32), axis=1),
                          N_EXPERTS - 1)
    block_valid = jnp.clip(pad_start[block_e] + counts[block_e] - block_row0, 0, blk)
    tc = COMBINE_TILE
    dest_t = dest.reshape(TOP_K, n_tok // tc, tc).transpose(1, 0, 2).reshape(n_tok // tc, TOP_K * tc)
    n_rows_total = n_blocks * blk
    n_pad = n_rows_total - n_tok * TOP_K
    tail = padded - counts
    tail_end = jnp.cumsum(tail)
    gap_base = jnp.concatenate([pad_start + counts - (tail_end - tail), (pad_end[-1] - tail_end[-1])[None]])
    p = jnp.arange(n_pad, dtype=jnp.int32)
    gap = jnp.sum((tail_end[None, :] <= p[:, None]).astype(jnp.int32), axis=1)
    pad_rows = p + jnp.sum(jnp.where(gap[:, None] == jnp.arange(N_EXPERTS + 1)[None, :], gap_base[None, :], 0), axis=1)
    td = DISPATCH_TILE
    dest_d = dest.reshape(TOP_K, n_tok // td, td).transpose(1, 0, 2).reshape(n_tok // td, TOP_K * td)
    xs = _dispatch(h2, dest_d, pad_rows, n_rows_total)
    ys = _experts(xs, block_e, block_valid, layer, *ew)
    return _combine(dest_t, ys, gates[0:TOP_K].T, x1, mod, final_g, mod_row, final)


def _layer_params(names, tensors, l):
    return {k: tensors[k][l] for k in names}


def kernel(x_prompt, x_sample, state_rwkv, state_lru, c, c_ctx, norm1_g, norm2_g, final_g, w_mod, b_mod, w_in, w_out,
           rw_mu, rw_w0, rw_w2, rw_a0, rw_a2, rw_g2, rw_kk, rw_ka, rw_rk, rw_lnx_g, rw_lnx_b,
           hy_conv_w, hy_conv_b, hy_f_w1, hy_f_b1, hy_f_w2, hy_f_b2, hy_f_w3, hy_f_b3,
           hy_freq, hy_decay, hy_bias, hy_out_g,
           lru_conv_w, lru_conv_b, lru_wa, lru_ba, lru_wx, lru_bx, lru_lam, lru_out_g,
           router_w, router_b, exp_w_gate, exp_b_gate, exp_w_up, exp_b_up, exp_w_down, exp_b_down):
    tensors = dict(locals())
    bp, sp, d = x_prompt.shape
    bl, sl, _ = x_sample.shape
    depth = w_in.shape[0]
    n_ctx, n_lat = bp * sp, bl * sl
    n_tok = n_ctx + n_lat
    assert sp % SEQ_BLOCK == 0 and sl % SEQ_BLOCK == 0 and n_ctx % sl == 0 and SEQ_BLOCK % GRID_W == 0
    assert n_tok % ROUTE_TILE == 0 and (n_tok * TOP_K) % EXPERT_ROWS == 0

    x = jnp.concatenate([x_prompt.reshape(n_ctx, d), x_sample.reshape(n_lat, d)], axis=0)
    n_rows = 16
    cc = jnp.concatenate([c_ctx[None, :], c, jnp.zeros((n_rows - 1 - bl, d), F32)], axis=0)
    mods = _modulation(cc, w_mod, b_mod).reshape(depth, n_rows, 6, d)
    ctx_tiles = n_ctx // TOK_TILE
    lat_tiles = sl // TOK_TILE
    mod_row = lambda i: jnp.where(i < ctx_tiles, 0, 1 + (i - ctx_tiles) // lat_tiles)

    ew = (exp_w_gate.astype(BF16), exp_b_gate[:, :, None, :], exp_w_up.astype(BF16), exp_b_up[:, :, None, :],
          exp_w_down.astype(BF16), exp_b_down[:, :, None, :])
    zero_rw = jnp.zeros((bp, 2, N_HEADS_A, HEAD, HEAD), F32)
    zero_lru = jnp.zeros((bp, 2, D_C), F32)
    nb_c, nb_l = sp // SEQ_BLOCK, sl // SEQ_BLOCK
    off_l = n_ctx // SEQ_BLOCK
    layer_names = [k for k, t in tensors.items() if k not in ("x_prompt", "x_sample", "state_rwkv", "state_lru", "c",
                                                              "c_ctx", "final_g", "w_mod", "b_mod", "exp_w_gate",
                                                              "exp_b_gate", "exp_w_up", "exp_b_up", "exp_w_down",
                                                              "exp_b_down")]
    new_rw, new_lru = [], []
    for l in range(depth):
        lp = _layer_params(layer_names, tensors, l)
        p = _in_proj(x, mods[l], lp["norm1_g"], lp["w_in"].astype(BF16), mod_row)
        rww, lrw, hyw = _rw_weights(lp), _lru_weights(lp), _hy_weights(lp)
        ya_c, s_rw = _rwkv_parts(p, rww, zero_rw, 0, bp, nb_c, sp)
        ya_l, _ = _rwkv_parts(p, rww, state_rwkv[:, l], off_l, bl, nb_l, GRID_W)
        yb_c = _hyena_mixer(p, hyw, 0, bp, sp, sp)
        yb_l = _hyena_mixer(p, hyw, n_ctx // sl, bl, sl, GRID_W)
        hf_c, hb_c, s_lru = _lru_mixer(p, lrw, zero_lru, 0, bp, nb_c, sp)
        hf_l, hb_l, _ = _lru_mixer(p, lrw, state_lru[:, l], off_l, bl, nb_l, GRID_W)
        cat = lambda a, b: jnp.concatenate([a, b], axis=0)
        x1, h2, logits_t = _out_proj([cat(a, b) for a, b in zip(ya_c, ya_l)], cat(yb_c, yb_l), cat(hf_c, hf_l),
                                     cat(hb_c, hb_l), p, x, mods[l], lp, mod_row)
        x = _moe(h2, logits_t, x1, mods[l], mod_row, l, ew, final_g, l == depth - 1)
        new_rw.append(s_rw)
        new_lru.append(s_lru)
    y_prompt = x[0:n_ctx].reshape(bp, sp, d)
    y_sample = x[n_ctx:].reshape(bl, sl, d)
    return (y_prompt, y_sample, jnp.stack(new_rw, axis=1), jnp.stack(new_lru, axis=1))
```
